```python
import jax
import jax.numpy as jnp
from jax import lax
import numpy as np

D_MODEL = 1024
BATCH = 8
SEQ = 4096
DEPTH = 1

CHUNK = 64
N_META = 16
LN_EPS = 1e-5
RMS_EPS = 1e-6

HG_HEADS = 8
HG_DK = 128
HG_DV = D_MODEL // HG_HEADS
HG_K_WIDTH = HG_HEADS * HG_DK
HG_V_WIDTH = HG_HEADS * HG_DV

LRU_WIDTH = D_MODEL
LRU_BLOCKS = 8
LRU_BLOCK = LRU_WIDTH // LRU_BLOCKS
CONV_W = 4
LRU_C = 8.0

SPLIT_SIZES = (HG_K_WIDTH, HG_K_WIDTH, HG_V_WIDTH, HG_V_WIDTH, LRU_WIDTH, LRU_WIDTH, D_MODEL, D_MODEL)
IN_COLS = 2 * HG_K_WIDTH + 2 * HG_V_WIDTH + 2 * LRU_WIDTH + 2 * D_MODEL

N_EXPERTS = 64
TOP_K = 6
D_EXPERT = 256
D_SHARED = 256
ROUTED_SCALE = 2.5
EXPERT_BLOCK = 128

DEEPNORM_ALPHA = (2.0 * DEPTH) ** 0.25
DEEPNORM_BETA = (8.0 * DEPTH) ** -0.25

kernel_name = 'hybrid_hgrn2_rglru_moe_block'


def layer_norm(x, g, b):
    xf = x.astype(jnp.float32)
    mu = jnp.mean(xf, axis=-1, keepdims=True)
    var = jnp.mean(jnp.square(xf - mu), axis=-1, keepdims=True)
    y = (xf - mu) * lax.rsqrt(var + LN_EPS) * g.astype(jnp.float32) + b.astype(jnp.float32)
    return y.astype(x.dtype)


def hgrn2_mixer(q, f_pre, v, g_out, lb, norm_g):
    B, L, _ = q.shape
    f32 = jnp.float32
    f = lb + (1.0 - lb) * jax.nn.sigmoid(f_pre.astype(f32))
    log_f = jnp.log(f)
    k = 1.0 - f
    qf = jax.nn.silu(q.astype(f32))
    vf = v.astype(f32)
    pad = (-L) % CHUNK
    n_chunks = (L + pad) // CHUNK

    def to_chunks(t, d):
        t = jnp.pad(t, ((0, 0), (pad, 0), (0, 0)))
        return t.reshape(B, n_chunks, CHUNK, HG_HEADS, d).transpose(1, 0, 3, 2, 4)

    causal = jnp.tril(jnp.ones((CHUNK, CHUNK), dtype=bool))[:, :, None]

    def chunk_step(S, inp):
        qc, kc, vc, lfc = inp
        b = jnp.cumsum(lfc, axis=2)
        diff = b[:, :, :, None, :] - b[:, :, None, :, :]
        decay = jnp.exp(jnp.where(causal, diff, -jnp.inf))
        scores = jnp.einsum('bhtk,bhsk,bhtsk->bhts', qc, kc, decay)
        o = (jnp.einsum('bhts,bhsv->bhtv', scores, vc)
             + jnp.einsum('bhtk,bhkv->bhtv', qc * jnp.exp(b), S))
        b_last = b[:, :, -1:, :]
        S = (jnp.exp(b_last[:, :, 0, :])[..., None] * S
             + jnp.einsum('bhsk,bhsv->bhkv', kc * jnp.exp(b_last - b), vc))
        return S, o

    S0 = jnp.zeros((B, HG_HEADS, HG_DK, HG_DV), f32)
    _, o = lax.scan(chunk_step, S0, (to_chunks(qf, HG_DK), to_chunks(k, HG_DK),
                                     to_chunks(vf, HG_DV), to_chunks(log_f, HG_DK)))
    o = o.transpose(1, 0, 3, 2, 4).reshape(B, n_chunks * CHUNK, HG_HEADS, HG_DV)[:, pad:]
    o = o * lax.rsqrt(jnp.mean(jnp.square(o), axis=-1, keepdims=True) + RMS_EPS) * norm_g.astype(f32)
    o = o.reshape(B, L, HG_V_WIDTH) * jax.nn.silu(g_out.astype(f32))
    return o.astype(q.dtype)


def causal_depthwise_conv(x, w, b):
    y = lax.conv_general_dilated(x, w[:, None, :], window_strides=(1,), padding=[(CONV_W - 1, 0)],
                                 dimension_numbers=('NWC', 'WIO', 'NWC'), feature_group_count=x.shape[-1])
    return y + b


def block_diag_linear(x, w, b):
    xb = x.reshape(x.shape[:-1] + (LRU_BLOCKS, LRU_BLOCK))
    return jnp.einsum('blnc,ncd->blnd', xb, w).reshape(x.shape) + b


def rg_lru(x, w_a, b_a, w_x, b_x, lam):
    f32 = jnp.float32
    r = jax.nn.sigmoid(block_diag_linear(x, w_a, b_a).astype(f32))
    i = jax.nn.sigmoid(block_diag_linear(x, w_x, b_x).astype(f32))
    log_a = -LRU_C * r * jax.nn.softplus(-lam.astype(f32))
    a = jnp.exp(log_a)
    u = jnp.sqrt(-jnp.expm1(2.0 * log_a)) * (i * x.astype(f32))

    def combine(c1, c2):
        a1, u1 = c1
        a2, u2 = c2
        return a1 * a2, a2 * u1 + u2

    _, h = lax.associative_scan(combine, (a, u), axis=1)
    return h.astype(x.dtype)


def hybrid_mixer(h, w_in, b_in, lb, hg_norm_g, conv_w, conv_b, lru_wa, lru_ba, lru_wx, lru_bx, lru_lambda, w_out):
    proj = h @ w_in + b_in
    offs = np.cumsum(SPLIT_SIZES)[:-1].tolist()
    q, f_pre, v, g_out, lru_x, lru_g, gate_hg, gate_lru = jnp.split(proj, offs, axis=-1)
    y_hg = hgrn2_mixer(q, f_pre, v, g_out, lb, hg_norm_g)
    y_lru = rg_lru(causal_depthwise_conv(lru_x, conv_w, conv_b), lru_wa, lru_ba, lru_wx, lru_bx,
                   lru_lambda) * jax.nn.gelu(lru_g)
    merged = jax.nn.sigmoid(gate_hg) * y_hg + jax.nn.sigmoid(gate_lru) * y_lru
    return merged @ w_out


def swiglu(x, w_gate, w_up, w_down):
    return (jax.nn.silu(x @ w_gate) * (x @ w_up)) @ w_down


def routed_experts(xt, idx, wts, we_gate, we_up, we_down):
    T, D = xt.shape
    A = T * TOP_K
    n_blocks = -(-A // EXPERT_BLOCK) + N_EXPERTS
    e_flat = idx.reshape(A)
    tok_flat = jnp.arange(A, dtype=jnp.int32) // TOP_K
    order = jnp.argsort(e_flat)
    e_sorted = e_flat[order]
    counts = jnp.bincount(e_flat, length=N_EXPERTS)
    starts = jnp.cumsum(counts) - counts
    padded = (counts + EXPERT_BLOCK - 1) // EXPERT_BLOCK * EXPERT_BLOCK
    pad_ends = jnp.cumsum(padded)
    dest = (pad_ends - padded)[e_sorted] + jnp.arange(A, dtype=counts.dtype) - starts[e_sorted]
    buf_tok = jnp.full((n_blocks * EXPERT_BLOCK,), T, jnp.int32).at[dest].set(tok_flat[order])
    buf_w = jnp.zeros((n_blocks * EXPERT_BLOCK,), wts.dtype).at[dest].set(wts.reshape(A)[order])
    block_start = jnp.arange(n_blocks, dtype=pad_ends.dtype) * EXPERT_BLOCK
    block_expert = jnp.minimum(jnp.searchsorted(pad_ends, block_start, side='right'), N_EXPERTS - 1)
    x_pad = jnp.concatenate([xt, jnp.zeros((1, D), xt.dtype)], axis=0)

    def block_step(acc, inp):
        e, tok_b, w_b = inp
        y_b = swiglu(x_pad[tok_b], we_gate[e], we_up[e], we_down[e])
        return acc.at[tok_b].add(y_b * w_b[:, None]), None

    acc, _ = lax.scan(block_step, jnp.zeros((T + 1, D), xt.dtype),
                      (block_expert, buf_tok.reshape(n_blocks, EXPERT_BLOCK),
                       buf_w.reshape(n_blocks, EXPERT_BLOCK)))
    return acc[:T]


def moe_ffn(h, router_w, router_bias, we_gate, we_up, we_down, ws_gate, ws_up, ws_down):
    B, L, D = h.shape
    xt = h.reshape(B * L, D)
    scores = jax.nn.sigmoid((xt @ router_w).astype(jnp.float32))
    _, idx = lax.top_k(scores + router_bias.astype(jnp.float32), TOP_K)
    s_sel = jnp.take_along_axis(scores, idx, axis=-1)
    wts = s_sel / jnp.sum(s_sel, axis=-1, keepdims=True) * ROUTED_SCALE
    y = routed_experts(xt, idx, wts.astype(xt.dtype), we_gate, we_up, we_down) + swiglu(xt, ws_gate, ws_up, ws_down)
    return y.reshape(B, L, D)


def setup_inputs(seed: int = 0) -> dict:
    key = jax.random.key(seed)
    ks = jax.random.split(key, 28)
    f32 = jnp.float32

    def nrm(k, shape, scale):
        return jax.random.normal(k, shape, f32) * scale

    d_inv = D_MODEL ** -0.5
    u2 = jax.random.uniform(ks[14], (DEPTH, LRU_WIDTH), f32, 0.81, 0.998)
    log_sig = 0.5 * jnp.log(u2) / LRU_C
    lru_lambda = log_sig - jnp.log(-jnp.expm1(log_sig))
    return {
        'x': nrm(ks[0], (BATCH, SEQ, D_MODEL), 1.0),
        'meta_tokens': nrm(ks[1], (N_META, D_MODEL), 1.0),
        'ln_in_g': 1.0 + nrm(ks[2], (D_MODEL,), 0.02),
        'ln_in_b': nrm(ks[3], (D_MODEL,), 0.02),
        'w_in': nrm(ks[4], (DEPTH, D_MODEL, IN_COLS), d_inv),
        'b_in': nrm(ks[5], (DEPTH, IN_COLS), 0.02),
        'hg_lb_logits': nrm(ks[6], (DEPTH + 1, HG_K_WIDTH), 0.5),
        'hg_norm_g': 1.0 + nrm(ks[7], (DEPTH, HG_DV), 0.02),
        'conv_w': nrm(ks[8], (DEPTH, CONV_W, LRU_WIDTH), CONV_W ** -0.5),
        'conv_b': nrm(ks[9], (DEPTH, LRU_WIDTH), 0.02),
        'lru_wa': nrm(ks[10], (DEPTH, LRU_BLOCKS, LRU_BLOCK, LRU_BLOCK), LRU_BLOCK ** -0.5),
        'lru_ba': nrm(ks[11], (DEPTH, LRU_WIDTH), 0.02),
        'lru_wx': nrm(ks[12], (DEPTH, LRU_BLOCKS, LRU_BLOCK, LRU_BLOCK), LRU_BLOCK ** -0.5),
        'lru_bx': nrm(ks[13], (DEPTH, LRU_WIDTH), 0.02),
        'lru_lambda': lru_lambda,
        'w_out': nrm(ks[15], (DEPTH, D_MODEL, D_MODEL), DEEPNORM_BETA * d_inv),
        'ln1_g': 1.0 + nrm(ks[16], (DEPTH, D_MODEL), 0.02),
        'ln1_b': nrm(ks[17], (DEPTH, D_MODEL), 0.02),
        'router_w': nrm(ks[18], (DEPTH, D_MODEL, N_EXPERTS), d_inv),
        'router_bias': nrm(ks[19], (DEPTH, N_EXPERTS), 0.01),
        'we_gate': nrm(ks[20], (DEPTH, N_EXPERTS, D_MODEL, D_EXPERT), d_inv),
        'we_up': nrm(ks[21], (DEPTH, N_EXPERTS, D_MODEL, D_EXPERT), d_inv),
        'we_down': nrm(ks[22], (DEPTH, N_EXPERTS, D_EXPERT, D_MODEL), DEEPNORM_BETA * D_EXPERT ** -0.5),
        'ws_gate': nrm(ks[23], (DEPTH, D_MODEL, D_SHARED), d_inv),
        'ws_up': nrm(ks[24], (DEPTH, D_MODEL, D_SHARED), d_inv),
        'ws_down': nrm(ks[25], (DEPTH, D_SHARED, D_MODEL), DEEPNORM_BETA * D_SHARED ** -0.5),
        'ln2_g': 1.0 + nrm(ks[26], (DEPTH, D_MODEL), 0.02),
        'ln2_b': nrm(ks[27], (DEPTH, D_MODEL), 0.02),
    }


def reference(x, meta_tokens, ln_in_g, ln_in_b, w_in, b_in, hg_lb_logits, hg_norm_g, conv_w, conv_b,
              lru_wa, lru_ba, lru_wx, lru_bx, lru_lambda, w_out, ln1_g, ln1_b, router_w, router_bias,
              we_gate, we_up, we_down, ws_gate, ws_up, ws_down, ln2_g, ln2_b):
    B = x.shape[0]
    meta = jnp.broadcast_to(meta_tokens[None].astype(x.dtype), (B, N_META, D_MODEL))
    h = layer_norm(jnp.concatenate([meta, x], axis=1), ln_in_g, ln_in_b)
    lower_bounds = jnp.cumsum(jax.nn.softmax(hg_lb_logits.astype(jnp.float32), axis=0), axis=0)
    for l in range(DEPTH):
        mix = hybrid_mixer(h, w_in[l], b_in[l], lower_bounds[l], hg_norm_g[l], conv_w[l], conv_b[l],
                           lru_wa[l], lru_ba[l], lru_wx[l], lru_bx[l], lru_lambda[l], w_out[l])
        h = layer_norm(DEEPNORM_ALPHA * h + mix, ln1_g[l], ln1_b[l])
        ffn = moe_ffn(h, router_w[l], router_bias[l], we_gate[l], we_up[l], we_down[l],
                      ws_gate[l], ws_up[l], ws_down[l])
        h = layer_norm(DEEPNORM_ALPHA * h + ffn, ln2_g[l], ln2_b[l])
    return h[:, N_META:]
```

```python
import functools

import jax
import jax.numpy as jnp
from jax import lax
from jax.experimental import pallas as pl
from jax.experimental.pallas import tpu as pltpu

F32 = jnp.float32
BF16 = jnp.bfloat16

N_META = 16
LN_EPS = 1e-5
RMS_EPS = 1e-6
HG_HEADS = 8
HEAD_DIM = 128
LRU_BLOCKS = 8
LRU_BLOCK = 128
CONV_W = 4
LRU_C = 8.0
TOP_K = 6
ROUTED_SCALE = 2.5
DEPTH = 1
DEEPNORM_ALPHA = (2.0 * DEPTH) ** 0.25

ROW_BLOCK = 128
ROUTER_TILE = 512
DISPATCH_TILE = 512
COMBINE_TILE = 256
EXPERT_ROWS = 128
SLOTS = 8
VMEM_LIMIT = 56 * 1024 * 1024


def _layer_norm(x, g, b):
    mu = jnp.mean(x, axis=-1, keepdims=True)
    xc = x - mu
    var = jnp.mean(xc * xc, axis=-1, keepdims=True)
    return xc * lax.rsqrt(var + LN_EPS) * g + b


def _sigmoid(x):
    return 1.0 / (1.0 + jnp.exp(-x))


def _silu(x):
    return x * _sigmoid(x)


def _gelu_tanh(x):
    return 0.5 * x * (1.0 + jnp.tanh(0.7978845608028654 * (x + 0.044715 * (x * x * x))))


def _dot(a, b):
    return jnp.dot(a, b, preferred_element_type=F32)


def _dot_nt(a, b):
    return lax.dot_general(a, b, (((1,), (1,)), ((), ())), preferred_element_type=F32)


def _split3(x):
    hi = x.astype(BF16)
    r1 = x - hi.astype(F32)
    mid = r1.astype(BF16)
    lo = (r1 - mid.astype(F32)).astype(BF16)
    return hi, mid, lo


def _linear_scan(a, u):
    rows = a.shape[0]
    rid = lax.broadcasted_iota(jnp.int32, (rows, 1), 0)
    d = 1
    while d < rows:
        keep = rid >= d
        a_sh = jnp.where(keep, pltpu.roll(a, d, 0), 1.0)
        u_sh = jnp.where(keep, pltpu.roll(u, d, 0), 0.0)
        u = u + a * u_sh
        a = a * a_sh
        d *= 2
    return a, u


def _mixer_kernel(x_ref, meta_ref, lng_ref, lnb_ref, win_ref, bin_ref, lbl_ref, hgn_ref, cw_ref, cb_ref,
                  wlru_ref, ba_ref, bx_ref, lam_ref, wout_ref, l1g_ref, l1b_ref,
                  h1_ref,
                  st_ref, cwin_ref, hcar_ref, yhg_ref):
    i = pl.program_id(1)
    rows = x_ref.shape[0]
    d_model = x_ref.shape[1]

    @pl.when(i == 0)
    def _():
        st_ref[...] = jnp.zeros_like(st_ref)
        cwin_ref[0:8, :] = jnp.zeros((8, d_model), F32)
        hcar_ref[...] = jnp.zeros_like(hcar_ref)

    rid = lax.broadcasted_iota(jnp.int32, (rows, 1), 0)
    is_first = i == 0
    valid = jnp.logical_or(rid >= rows - N_META, jnp.logical_not(is_first))
    xin = jnp.where(is_first, meta_ref[...], x_ref[...])
    h = _layer_norm(xin, lng_ref[...], lnb_ref[...])
    hb = h.astype(BF16)

    def proj(seg):
        cols = slice(seg * d_model, (seg + 1) * d_model)
        return _dot(hb, win_ref[:, cols]) + bin_ref[:, cols]

    lbl = lbl_ref[...]
    lmax = jnp.max(lbl, axis=0, keepdims=True)
    lexp = jnp.exp(lbl - lmax)
    lb = lexp[0:1, :] / jnp.sum(lexp, axis=0, keepdims=True)

    q = _silu(proj(0))
    f = lb + (1.0 - lb) * _sigmoid(proj(1))
    lf = jnp.where(valid, jnp.log(f), 0.0)
    k = jnp.where(valid, 1.0 - f, 0.0)

    r_i = lax.broadcasted_iota(jnp.int32, (rows, rows), 0)
    c_i = lax.broadcasted_iota(jnp.int32, (rows, rows), 1)
    causal = r_i >= c_i
    tri = jnp.where(causal, 1.0, 0.0).astype(BF16)
    lf_hi, lf_mid, lf_lo = _split3(lf)
    b = _dot(tri, lf_hi) + _dot(tri, lf_mid) + _dot(tri, lf_lo)
    half = rows // 2
    b_mid = b[half - 1:half, :]
    b_last = b[rows - 1:rows, :]
    e_fwd = jnp.exp(b - b_mid)
    e_bwd = jnp.exp(b_mid - b)
    q_t = q * e_fwd
    k_t = k * e_bwd
    q_s = (q_t * jnp.exp(b_mid)).astype(BF16)
    k_s = (k_t * jnp.exp(b_last - b_mid)).astype(BF16)
    q_t = q_t.astype(BF16)
    k_t = k_t.astype(BF16)
    s_decay = jnp.exp(b_last)
    v = proj(2)
    vb = v.astype(BF16)
    hgn = hgn_ref[...]
    for hd in range(HG_HEADS):
        cs = slice(hd * HEAD_DIM, (hd + 1) * HEAD_DIM)
        p = jnp.where(causal, _dot_nt(q_t[:, cs], k_t[:, cs]), 0.0)
        st = st_ref[hd]
        o = _dot(p.astype(BF16), vb[:, cs]) + _dot_nt(q_s[:, cs], st.astype(BF16))
        st_ref[hd] = st * s_decay[:, cs] + _dot(v[:, cs].T.astype(BF16), k_s[:, cs])
        o = o * lax.rsqrt(jnp.mean(o * o, axis=-1, keepdims=True) + RMS_EPS) * hgn
        yhg_ref[:, cs] = o
    y_hg = yhg_ref[...] * _silu(proj(3))

    lx = jnp.where(valid, proj(4), 0.0)
    cwin_ref[8:8 + rows, :] = lx
    cw = cw_ref[...]
    xc = cb_ref[...] + cw[0:1, :] * cwin_ref[5:5 + rows, :]
    for j in range(1, CONV_W):
        xc = xc + cw[j:j + 1, :] * cwin_ref[5 + j:5 + j + rows, :]
    cwin_ref[0:8, :] = lx[rows - 8:rows, :]

    ra_parts, ri_parts = [], []
    for n in range(LRU_BLOCKS):
        cs = slice(n * LRU_BLOCK, (n + 1) * LRU_BLOCK)
        g2 = _dot(xc[:, cs].astype(BF16), wlru_ref[n])
        ra_parts.append(g2[:, :LRU_BLOCK])
        ri_parts.append(g2[:, LRU_BLOCK:])
    r_gate = _sigmoid(jnp.concatenate(ra_parts, axis=1) + ba_ref[...])
    i_gate = _sigmoid(jnp.concatenate(ri_parts, axis=1) + bx_ref[...])
    lam = lam_ref[...]
    neg = -lam
    softplus_neg = jnp.maximum(neg, 0.0) + jnp.log(1.0 + jnp.exp(-jnp.abs(neg)))
    log_a = -LRU_C * r_gate * softplus_neg
    a = jnp.exp(log_a)
    u = jnp.where(valid, jnp.sqrt(1.0 - a * a) * (i_gate * xc), 0.0)
    a_cum, u_cum = _linear_scan(a, u)
    h_lru = u_cum + a_cum * hcar_ref[0:1, :]
    hcar_ref[0:1, :] = h_lru[rows - 1:rows, :]
    y_lru = h_lru * _gelu_tanh(proj(5))

    merged = _sigmoid(proj(6)) * y_hg + _sigmoid(proj(7)) * y_lru
    mix = _dot(merged.astype(BF16), wout_ref[...])
    h1_ref[...] = _layer_norm(DEEPNORM_ALPHA * h + mix, l1g_ref[...], l1b_ref[...])


def _const_spec(shape):
    nd = len(shape)
    return pl.BlockSpec(shape, lambda b, i: (0,) * nd, pipeline_mode=pl.Buffered(1))


def _mixer(x, meta_pad, ln_g, ln_b, w_in, b_in, lb_logits, hg_norm, conv_w, conv_b, w_lru, ba, bx, lam,
           w_out, l1g, l1b):
    batch, seq, d_model = x.shape
    rows = ROW_BLOCK
    assert seq % rows == 0 and d_model == HG_HEADS * HEAD_DIM == LRU_BLOCKS * LRU_BLOCK
    n_blk = seq // rows + 1
    consts = (meta_pad, ln_g, ln_b, w_in, b_in, lb_logits, hg_norm, conv_w, conv_b, w_lru, ba, bx, lam,
              w_out, l1g, l1b)
    x_spec = pl.BlockSpec((None, rows, d_model), lambda b, i: (b, jnp.maximum(i - 1, 0), 0))
    return pl.pallas_call(
        _mixer_kernel,
        grid=(batch, n_blk),
        in_specs=[x_spec] + [_const_spec(c.shape) for c in consts],
        out_specs=pl.BlockSpec((None, rows, d_model), lambda b, i: (b, jnp.maximum(i - 1, 0), 0)),
        out_shape=jax.ShapeDtypeStruct((batch, seq, d_model), F32),
        scratch_shapes=[
            pltpu.VMEM((HG_HEADS, HEAD_DIM, HEAD_DIM), F32),
            pltpu.VMEM((rows + 8, d_model), F32),
            pltpu.VMEM((8, d_model), F32),
            pltpu.VMEM((rows, d_model), F32),
        ],
        compiler_params=pltpu.CompilerParams(
            dimension_semantics=("arbitrary", "arbitrary"), vmem_limit_bytes=VMEM_LIMIT),
        name="mixer",
    )(x, *consts)


def _router_kernel(h_ref, rw_ref, rb_ref, e_ref, r_ref, w_ref, cnt_ref):
    step = pl.program_id(0)

    @pl.when(step == 0)
    def _():
        cnt_ref[...] = jnp.zeros_like(cnt_ref)

    h = h_ref[...]
    tt = h.shape[0]
    n_exp = rw_ref.shape[1]
    h_hi = h.astype(BF16)
    h_lo = (h - h_hi.astype(F32)).astype(BF16)
    rw = rw_ref[...]
    w_hi = rw.astype(BF16)
    w_lo = (rw - w_hi.astype(F32)).astype(BF16)
    logits = _dot(h_hi, w_hi) + _dot(h_hi, w_lo) + _dot(h_lo, w_hi)
    scores = _sigmoid(logits)
    sel = scores + rb_ref[...]
    lane = lax.broadcasted_iota(jnp.int32, (tt, n_exp), 1)
    onehots = []
    for _ in range(TOP_K):
        m = jnp.max(sel, axis=-1, keepdims=True)
        idx = jnp.min(jnp.where(sel == m, lane, n_exp), axis=-1, keepdims=True)
        oh = lane == idx
        onehots.append(oh)
        sel = jnp.where(oh, -jnp.inf, sel)
    chosen = onehots[0]
    for oh in onehots[1:]:
        chosen = jnp.logical_or(chosen, oh)
    chosen_f = jnp.where(chosen, 1.0, 0.0)
    r_i = lax.broadcasted_iota(jnp.int32, (tt, tt), 0)
    c_i = lax.broadcasted_iota(jnp.int32, (tt, tt), 1)
    strict = jnp.where(r_i > c_i, 1.0, 0.0).astype(BF16)
    counts = cnt_ref[...]
    rank = _dot(strict, chosen_f.astype(BF16)) + counts
    cnt_ref[...] = counts + jnp.sum(chosen_f, axis=0, keepdims=True)

    slot = lax.broadcasted_iota(jnp.int32, (tt, SLOTS), 1)
    s_cols = [jnp.sum(jnp.where(oh, scores, 0.0), axis=-1, keepdims=True) for oh in onehots]
    denom = s_cols[0]
    for s in s_cols[1:]:
        denom = denom + s
    e_tab = jnp.zeros((tt, SLOTS), jnp.int32)
    r_tab = jnp.zeros((tt, SLOTS), jnp.int32)
    w_tab = jnp.zeros((tt, SLOTS), F32)
    for kk, oh in enumerate(onehots):
        e_col = jnp.sum(jnp.where(oh, lane, 0), axis=-1, keepdims=True)
        r_col = jnp.sum(jnp.where(oh, rank, 0.0), axis=-1, keepdims=True).astype(jnp.int32)
        w_col = s_cols[kk] / denom * ROUTED_SCALE
        e_tab = jnp.where(slot == kk, e_col, e_tab)
        r_tab = jnp.where(slot == kk, r_col, r_tab)
        w_tab = jnp.where(slot == kk, w_col, w_tab)
    e_ref[...] = e_tab
    r_ref[...] = r_tab
    w_ref[...] = w_tab


def _router(h1, router_w, router_b):
    tokens, d_model = h1.shape
    n_exp = router_w.shape[1]
    tt = ROUTER_TILE
    assert tokens % tt == 0
    tab = lambda dt: jax.ShapeDtypeStruct((tokens, SLOTS), dt)
    tab_spec = pl.BlockSpec((tt, SLOTS), lambda t: (t, 0))
    return pl.pallas_call(
        _router_kernel,
        grid=(tokens // tt,),
        in_specs=[pl.BlockSpec((tt, d_model), lambda t: (t, 0)),
                  pl.BlockSpec((d_model, n_exp), lambda t: (0, 0)),
                  pl.BlockSpec((1, n_exp), lambda t: (0, 0))],
        out_specs=[tab_spec, tab_spec, tab_spec, pl.BlockSpec((1, n_exp), lambda t: (0, 0))],
        out_shape=[tab(jnp.int32), tab(jnp.int32), tab(F32), jax.ShapeDtypeStruct((1, n_exp), F32)],
        compiler_params=pltpu.CompilerParams(dimension_semantics=("arbitrary",), vmem_limit_bytes=VMEM_LIMIT),
        name="router",
    )(h1, router_w, router_b)


def _dispatch_kernel(pos_ref, h_ref, xs_in_ref, xs_ref, sem):
    del xs_in_ref
    tt = h_ref.shape[0]

    def row_copy(t, p):
        return pltpu.make_async_copy(h_ref.at[pl.ds(t, 1), :], xs_ref.at[pl.ds(p, 1), :], sem)

    def issue(t, carry):
        for kk in range(TOP_K):
            row_copy(t, pos_ref[t * SLOTS + kk]).start()
        return carry

    lax.fori_loop(0, tt, issue, 0)

    def drain(t, carry):
        for kk in range(TOP_K):
            row_copy(t, pos_ref[t * SLOTS + kk]).wait()
        return carry

    lax.fori_loop(0, tt, drain, 0)


def _dispatch(pos_flat, h1, xs_init):
    tokens, d_model = h1.shape
    tt = DISPATCH_TILE
    assert tokens % tt == 0
    return pl.pallas_call(
        _dispatch_kernel,
        grid=(tokens // tt,),
        in_specs=[pl.BlockSpec((tt * SLOTS,), lambda t: (t,), memory_space=pltpu.SMEM),
                  pl.BlockSpec((tt, d_model), lambda t: (t, 0)),
                  pl.BlockSpec(memory_space=pl.ANY)],
        out_specs=pl.BlockSpec(memory_space=pl.ANY),
        out_shape=jax.ShapeDtypeStruct(xs_init.shape, xs_init.dtype),
        scratch_shapes=[pltpu.SemaphoreType.DMA],
        input_output_aliases={2: 0},
        compiler_params=pltpu.CompilerParams(dimension_semantics=("arbitrary",), vmem_limit_bytes=VMEM_LIMIT),
        name="dispatch",
    )(pos_flat, h1, xs_init)


def _expert_kernel(be_ref, na_ref, xs_ref, wgu_ref, wd_ref, ys_ref):
    del be_ref
    j = pl.program_id(0)
    d_exp = wd_ref.shape[0]

    @pl.when(j < na_ref[0])
    def _():
        gu = _dot(xs_ref[...].astype(BF16), wgu_ref[...])
        act = _silu(gu[:, :d_exp]) * gu[:, d_exp:]
        ys_ref[...] = _dot(act.astype(BF16), wd_ref[...])


def _experts(block_expert, n_active, xs, w_gu, w_down):
    rows_total, d_model = xs.shape
    eb = EXPERT_ROWS
    n_blocks = rows_total // eb
    d_exp = w_down.shape[1]

    def row_map(j, be, na):
        return (jnp.minimum(j, na[0] - 1), 0)

    def w_map(j, be, na):
        return (be[j], 0, 0)

    grid_spec = pltpu.PrefetchScalarGridSpec(
        num_scalar_prefetch=2,
        grid=(n_blocks,),
        in_specs=[pl.BlockSpec((eb, d_model), row_map),
                  pl.BlockSpec((None, d_model, 2 * d_exp), w_map),
                  pl.BlockSpec((None, d_exp, d_model), w_map)],
        out_specs=pl.BlockSpec((eb, d_model), row_map),
    )
    return pl.pallas_call(
        _expert_kernel,
        grid_spec=grid_spec,
        out_shape=jax.ShapeDtypeStruct((rows_total, d_model), F32),
        compiler_params=pltpu.CompilerParams(dimension_semantics=("arbitrary",), vmem_limit_bytes=VMEM_LIMIT),
        name="experts",
    )(block_expert, n_active, xs, w_gu, w_down)


def _combine_kernel(pos_ref, h_ref, w_ref, wsgu_ref, wsd_ref, l2g_ref, l2b_ref, ys_ref, out_ref, gbuf, sem):
    tt = h_ref.shape[0]
    d_sh = wsd_ref.shape[0]

    def row_copy(t, kk, p):
        return pltpu.make_async_copy(ys_ref.at[pl.ds(p, 1), :], gbuf.at[kk, pl.ds(t, 1), :], sem)

    def issue(t, carry):
        for kk in range(TOP_K):
            row_copy(t, kk, pos_ref[t * SLOTS + kk]).start()
        return carry

    lax.fori_loop(0, tt, issue, 0)

    h = h_ref[...]
    gu = _dot(h.astype(BF16), wsgu_ref[...])
    act = _silu(gu[:, :d_sh]) * gu[:, d_sh:]
    acc = DEEPNORM_ALPHA * h + _dot(act.astype(BF16), wsd_ref[...])

    def drain(t, carry):
        for kk in range(TOP_K):
            row_copy(t, kk, pos_ref[t * SLOTS + kk]).wait()
        return carry

    lax.fori_loop(0, tt, drain, 0)

    w = w_ref[...]
    for kk in range(TOP_K):
        acc = acc + w[:, kk:kk + 1] * gbuf[kk]
    out_ref[...] = _layer_norm(acc, l2g_ref[...], l2b_ref[...])


def _combine(pos_flat, h1, w_tab, ws_gu, ws_down, l2g, l2b, ys):
    tokens, d_model = h1.shape
    tt = COMBINE_TILE
    assert tokens % tt == 0
    full = lambda a: pl.BlockSpec(a.shape, lambda t: (0,) * a.ndim)
    return pl.pallas_call(
        _combine_kernel,
        grid=(tokens // tt,),
        in_specs=[pl.BlockSpec((tt * SLOTS,), lambda t: (t,), memory_space=pltpu.SMEM),
                  pl.BlockSpec((tt, d_model), lambda t: (t, 0)),
                  pl.BlockSpec((tt, SLOTS), lambda t: (t, 0)),
                  full(ws_gu), full(ws_down), full(l2g), full(l2b),
                  pl.BlockSpec(memory_space=pl.ANY)],
        out_specs=pl.BlockSpec((tt, d_model), lambda t: (t, 0)),
        out_shape=jax.ShapeDtypeStruct((tokens, d_model), F32),
        scratch_shapes=[pltpu.VMEM((TOP_K, tt, d_model), F32), pltpu.SemaphoreType.DMA],
        compiler_params=pltpu.CompilerParams(dimension_semantics=("arbitrary",), vmem_limit_bytes=VMEM_LIMIT),
        name="combine",
    )(pos_flat, h1, w_tab, ws_gu, ws_down, l2g, l2b, ys)


def kernel(x, meta_tokens, ln_in_g, ln_in_b, w_in, b_in, hg_lb_logits, hg_norm_g, conv_w, conv_b, lru_wa, lru_ba,
           lru_wx, lru_bx, lru_lambda, w_out, ln1_g, ln1_b, router_w, router_bias, we_gate, we_up, we_down,
           ws_gate, ws_up, ws_down, ln2_g, ln2_b):
    batch, seq, d_model = x.shape
    assert w_in.shape[0] == DEPTH
    n_exp = router_w.shape[-1]
    row = lambda a: a.reshape(1, -1).astype(F32)

    meta_pad = jnp.concatenate([jnp.zeros((ROW_BLOCK - N_META, d_model), F32), meta_tokens.astype(F32)], axis=0)
    w_lru = jnp.concatenate([lru_wa[0], lru_wx[0]], axis=-1).astype(BF16)
    h1 = _mixer(x, meta_pad, row(ln_in_g), row(ln_in_b), w_in[0].astype(BF16), row(b_in[0]),
                hg_lb_logits.astype(F32), row(hg_norm_g[0]), conv_w[0].astype(F32), row(conv_b[0]), w_lru,
                row(lru_ba[0]), row(lru_bx[0]), row(lru_lambda[0]), w_out[0].astype(BF16),
                row(ln1_g[0]), row(ln1_b[0]))
    tokens = batch * seq
    h1 = h1.reshape(tokens, d_model)

    e_tab, r_tab, w_tab, counts = _router(h1, router_w[0].astype(F32), row(router_bias[0]))

    eb = EXPERT_ROWS
    n_blocks = tokens * TOP_K // eb + n_exp
    counts = counts.reshape(n_exp).astype(jnp.int32)
    padded = (counts + eb - 1) // eb * eb
    pad_ends = jnp.cumsum(padded)
    pad_starts = pad_ends - padded
    n_active = (pad_ends[-1] // eb).astype(jnp.int32)
    blk = jnp.minimum(jnp.arange(n_blocks, dtype=jnp.int32), n_active - 1)
    block_expert = jnp.minimum(jnp.searchsorted(pad_ends, blk * eb, side='right'), n_exp - 1).astype(jnp.int32)
    pos = pad_starts[e_tab] + r_tab
    pos_flat = pos.reshape(tokens * SLOTS).astype(jnp.int32)

    xs = _dispatch(pos_flat, h1, jnp.zeros((n_blocks * eb, d_model), F32))
    w_gu = jnp.concatenate([we_gate[0], we_up[0]], axis=-1).astype(BF16)
    ys = _experts(block_expert, n_active.reshape(1), xs, w_gu, we_down[0].astype(BF16))
    ws_gu = jnp.concatenate([ws_gate[0], ws_up[0]], axis=-1).astype(BF16)
    out = _combine(pos_flat, h1, w_tab, ws_gu, ws_down[0].astype(BF16), row(ln2_g[0]), row(ln2_b[0]), ys)
    return out.reshape(batch, seq, d_model)
```

```python
import functools

import jax
import jax.numpy as jnp
from jax import lax
from jax.experimental import pallas as pl
from jax.experimental.pallas import tpu as pltpu

F32 = jnp.float32
BF16 = jnp.bfloat16

N_META = 16
LN_EPS = 1e-5
RMS_EPS = 1e-6
HG_HEADS = 8
HEAD_DIM = 128
LRU_BLOCKS = 8
LRU_BLOCK = 128
CONV_W = 4
LRU_C = 8.0
TOP_K = 6
ROUTED_SCALE = 2.5
DEPTH = 1
DEEPNORM_ALPHA = (2.0 * DEPTH) ** 0.25

ROW_BLOCK = 128
ROUTER_TILE = 512
DISPATCH_TILE = 512
COMBINE_TILE = 256
EXPERT_ROWS = 256
SLOTS = 8
VMEM_LIMIT = 56 * 1024 * 1024


def _layer_norm(x, g, b):
    mu = jnp.mean(x, axis=-1, keepdims=True)
    xc = x - mu
    var = jnp.mean(xc * xc, axis=-1, keepdims=True)
    return xc * lax.rsqrt(var + LN_EPS) * g + b


def _sigmoid(x):
    return 1.0 / (1.0 + jnp.exp(-x))


def _silu(x):
    return x * _sigmoid(x)


def _gelu_tanh(x):
    return 0.5 * x * (1.0 + jnp.tanh(0.7978845608028654 * (x + 0.044715 * (x * x * x))))


def _dot(a, b):
    return jnp.dot(a, b, preferred_element_type=F32)


def _dot_nt(a, b):
    return lax.dot_general(a, b, (((1,), (1,)), ((), ())), preferred_element_type=F32)


def _split3(x):
    hi = x.astype(BF16)
    r1 = x - hi.astype(F32)
    mid = r1.astype(BF16)
    lo = (r1 - mid.astype(F32)).astype(BF16)
    return hi, mid, lo


def _linear_scan(a, u):
    rows = a.shape[0]
    rid = lax.broadcasted_iota(jnp.int32, (rows, 1), 0)
    d = 1
    while d < rows:
        keep = rid >= d
        a_sh = jnp.where(keep, pltpu.roll(a, d, 0), 1.0)
        u_sh = jnp.where(keep, pltpu.roll(u, d, 0), 0.0)
        u = u + a * u_sh
        a = a * a_sh
        d *= 2
    return a, u


def _mixer_kernel(x_ref, meta_ref, lng_ref, lnb_ref, win_ref, bin_ref, lbl_ref, hgn_ref, cw_ref, cb_ref,
                  wlru_ref, ba_ref, bx_ref, lam_ref, wout_ref, l1g_ref, l1b_ref,
                  h1_ref,
                  st_ref, cwin_ref, hcar_ref, yhg_ref):
    i = pl.program_id(1)
    rows = x_ref.shape[0]
    d_model = x_ref.shape[1]

    @pl.when(i == 0)
    def _():
        st_ref[...] = jnp.zeros_like(st_ref)
        cwin_ref[0:8, :] = jnp.zeros((8, d_model), F32)
        hcar_ref[...] = jnp.zeros_like(hcar_ref)

    rid = lax.broadcasted_iota(jnp.int32, (rows, 1), 0)
    is_first = i == 0
    valid = jnp.logical_or(rid >= rows - N_META, jnp.logical_not(is_first))
    xin = jnp.where(is_first, meta_ref[...], x_ref[...])
    h = _layer_norm(xin, lng_ref[...], lnb_ref[...])
    hb = h.astype(BF16)

    def proj(seg):
        cols = slice(seg * d_model, (seg + 1) * d_model)
        return _dot(hb, win_ref[:, cols]) + bin_ref[:, cols]

    lbl = lbl_ref[...]
    lmax = jnp.max(lbl, axis=0, keepdims=True)
    lexp = jnp.exp(lbl - lmax)
    lb = lexp[0:1, :] / jnp.sum(lexp, axis=0, keepdims=True)

    q = _silu(proj(0))
    f = lb + (1.0 - lb) * _sigmoid(proj(1))
    lf = jnp.where(valid, jnp.log(f), 0.0)
    k = jnp.where(valid, 1.0 - f, 0.0)

    r_i = lax.broadcasted_iota(jnp.int32, (rows, rows), 0)
    c_i = lax.broadcasted_iota(jnp.int32, (rows, rows), 1)
    causal = r_i >= c_i
    tri = jnp.where(causal, 1.0, 0.0).astype(BF16)
    lf_hi, lf_mid, lf_lo = _split3(lf)
    b = _dot(tri, lf_hi) + _dot(tri, lf_mid) + _dot(tri, lf_lo)
    half = rows // 2
    b_mid = b[half - 1:half, :]
    b_last = b[rows - 1:rows, :]
    e_fwd = jnp.exp(b - b_mid)
    e_bwd = jnp.exp(b_mid - b)
    q_t = q * e_fwd
    k_t = k * e_bwd
    q_s = (q_t * jnp.exp(b_mid)).astype(BF16)
    k_s = (k_t * jnp.exp(b_last - b_mid)).astype(BF16)
    q_t = q_t.astype(BF16)
    k_t = k_t.astype(BF16)
    s_decay = jnp.exp(b_last)
    v = proj(2)
    vb = v.astype(BF16)
    hgn = hgn_ref[...]
    for hd in range(HG_HEADS):
        cs = slice(hd * HEAD_DIM, (hd + 1) * HEAD_DIM)
        p = jnp.where(causal, _dot_nt(q_t[:, cs], k_t[:, cs]), 0.0)
        st = st_ref[hd]
        o = _dot(p.astype(BF16), vb[:, cs]) + _dot_nt(q_s[:, cs], st.astype(BF16))
        st_ref[hd] = st * s_decay[:, cs] + _dot(v[:, cs].T.astype(BF16), k_s[:, cs])
        o = o * lax.rsqrt(jnp.mean(o * o, axis=-1, keepdims=True) + RMS_EPS) * hgn
        yhg_ref[:, cs] = o
    y_hg = yhg_ref[...] * _silu(proj(3))

    lx = jnp.where(valid, proj(4), 0.0)
    cwin_ref[8:8 + rows, :] = lx
    cw = cw_ref[...]
    xc = cb_ref[...] + cw[0:1, :] * cwin_ref[5:5 + rows, :]
    for j in range(1, CONV_W):
        xc = xc + cw[j:j + 1, :] * cwin_ref[5 + j:5 + j + rows, :]
    cwin_ref[0:8, :] = lx[rows - 8:rows, :]

    ra_parts, ri_parts = [], []
    for n in range(LRU_BLOCKS):
        cs = slice(n * LRU_BLOCK, (n + 1) * LRU_BLOCK)
        g2 = _dot(xc[:, cs].astype(BF16), wlru_ref[n])
        ra_parts.append(g2[:, :LRU_BLOCK])
        ri_parts.append(g2[:, LRU_BLOCK:])
    r_gate = _sigmoid(jnp.concatenate(ra_parts, axis=1) + ba_ref[...])
    i_gate = _sigmoid(jnp.concatenate(ri_parts, axis=1) + bx_ref[...])
    lam = lam_ref[...]
    neg = -lam
    softplus_neg = jnp.maximum(neg, 0.0) + jnp.log(1.0 + jnp.exp(-jnp.abs(neg)))
    log_a = -LRU_C * r_gate * softplus_neg
    a = jnp.exp(log_a)
    u = jnp.where(valid, jnp.sqrt(1.0 - a * a) * (i_gate * xc), 0.0)
    a_cum, u_cum = _linear_scan(a, u)
    h_lru = u_cum + a_cum * hcar_ref[0:1, :]
    hcar_ref[0:1, :] = h_lru[rows - 1:rows, :]
    y_lru = h_lru * _gelu_tanh(proj(5))

    merged = _sigmoid(proj(6)) * y_hg + _sigmoid(proj(7)) * y_lru
    mix = _dot(merged.astype(BF16), wout_ref[...])
    h1_ref[...] = _layer_norm(DEEPNORM_ALPHA * h + mix, l1g_ref[...], l1b_ref[...])


def _const_spec(shape):
    nd = len(shape)
    return pl.BlockSpec(shape, lambda b, i: (0,) * nd, pipeline_mode=pl.Buffered(1))


def _mixer(x, meta_pad, ln_g, ln_b, w_in, b_in, lb_logits, hg_norm, conv_w, conv_b, w_lru, ba, bx, lam,
           w_out, l1g, l1b):
    batch, seq, d_model = x.shape
    rows = ROW_BLOCK
    assert seq % rows == 0 and d_model == HG_HEADS * HEAD_DIM == LRU_BLOCKS * LRU_BLOCK
    n_blk = seq // rows + 1
    consts = (meta_pad, ln_g, ln_b, w_in, b_in, lb_logits, hg_norm, conv_w, conv_b, w_lru, ba, bx, lam,
              w_out, l1g, l1b)
    x_spec = pl.BlockSpec((None, rows, d_model), lambda b, i: (b, jnp.maximum(i - 1, 0), 0))
    return pl.pallas_call(
        _mixer_kernel,
        grid=(batch, n_blk),
        in_specs=[x_spec] + [_const_spec(c.shape) for c in consts],
        out_specs=pl.BlockSpec((None, rows, d_model), lambda b, i: (b, jnp.maximum(i - 1, 0), 0)),
        out_shape=jax.ShapeDtypeStruct((batch, seq, d_model), F32),
        scratch_shapes=[
            pltpu.VMEM((HG_HEADS, HEAD_DIM, HEAD_DIM), F32),
            pltpu.VMEM((rows + 8, d_model), F32),
            pltpu.VMEM((8, d_model), F32),
            pltpu.VMEM((rows, d_model), F32),
        ],
        compiler_params=pltpu.CompilerParams(
            dimension_semantics=("arbitrary", "arbitrary"), vmem_limit_bytes=VMEM_LIMIT),
        name="mixer",
    )(x, *consts)


def _router_kernel(h_ref, rw_ref, rb_ref, e_ref, r_ref, w_ref, cnt_ref):
    step = pl.program_id(0)

    @pl.when(step == 0)
    def _():
        cnt_ref[...] = jnp.zeros_like(cnt_ref)

    h = h_ref[...]
    tt = h.shape[0]
    n_exp = rw_ref.shape[1]
    h_hi = h.astype(BF16)
    h_lo = (h - h_hi.astype(F32)).astype(BF16)
    rw = rw_ref[...]
    w_hi = rw.astype(BF16)
    w_lo = (rw - w_hi.astype(F32)).astype(BF16)
    logits = _dot(h_hi, w_hi) + _dot(h_hi, w_lo) + _dot(h_lo, w_hi)
    scores = _sigmoid(logits)
    sel = scores + rb_ref[...]
    lane = lax.broadcasted_iota(jnp.int32, (tt, n_exp), 1)
    onehots = []
    for _ in range(TOP_K):
        m = jnp.max(sel, axis=-1, keepdims=True)
        idx = jnp.min(jnp.where(sel == m, lane, n_exp), axis=-1, keepdims=True)
        oh = lane == idx
        onehots.append(oh)
        sel = jnp.where(oh, -jnp.inf, sel)
    chosen = onehots[0]
    for oh in onehots[1:]:
        chosen = jnp.logical_or(chosen, oh)
    chosen_f = jnp.where(chosen, 1.0, 0.0)
    r_i = lax.broadcasted_iota(jnp.int32, (tt, tt), 0)
    c_i = lax.broadcasted_iota(jnp.int32, (tt, tt), 1)
    strict = jnp.where(r_i > c_i, 1.0, 0.0).astype(BF16)
    counts = cnt_ref[...]
    rank = _dot(strict, chosen_f.astype(BF16)) + counts
    cnt_ref[...] = counts + jnp.sum(chosen_f, axis=0, keepdims=True)

    slot = lax.broadcasted_iota(jnp.int32, (tt, SLOTS), 1)
    s_cols = [jnp.sum(jnp.where(oh, scores, 0.0), axis=-1, keepdims=True) for oh in onehots]
    denom = s_cols[0]
    for s in s_cols[1:]:
        denom = denom + s
    e_tab = jnp.zeros((tt, SLOTS), jnp.int32)
    r_tab = jnp.zeros((tt, SLOTS), jnp.int32)
    w_tab = jnp.zeros((tt, SLOTS), F32)
    for kk, oh in enumerate(onehots):
        e_col = jnp.sum(jnp.where(oh, lane, 0), axis=-1, keepdims=True)
        r_col = jnp.sum(jnp.where(oh, rank, 0.0), axis=-1, keepdims=True).astype(jnp.int32)
        w_col = s_cols[kk] / denom * ROUTED_SCALE
        e_tab = jnp.where(slot == kk, e_col, e_tab)
        r_tab = jnp.where(slot == kk, r_col, r_tab)
        w_tab = jnp.where(slot == kk, w_col, w_tab)
    e_ref[...] = e_tab
    r_ref[...] = r_tab
    w_ref[...] = w_tab


def _router(h1, router_w, router_b):
    tokens, d_model = h1.shape
    n_exp = router_w.shape[1]
    tt = ROUTER_TILE
    assert tokens % tt == 0
    tab = lambda dt: jax.ShapeDtypeStruct((tokens, SLOTS), dt)
    tab_spec = pl.BlockSpec((tt, SLOTS), lambda t: (t, 0))
    return pl.pallas_call(
        _router_kernel,
        grid=(tokens // tt,),
        in_specs=[pl.BlockSpec((tt, d_model), lambda t: (t, 0)),
                  pl.BlockSpec((d_model, n_exp), lambda t: (0, 0)),
                  pl.BlockSpec((1, n_exp), lambda t: (0, 0))],
        out_specs=[tab_spec, tab_spec, tab_spec, pl.BlockSpec((1, n_exp), lambda t: (0, 0))],
        out_shape=[tab(jnp.int32), tab(jnp.int32), tab(F32), jax.ShapeDtypeStruct((1, n_exp), F32)],
        compiler_params=pltpu.CompilerParams(dimension_semantics=("arbitrary",), vmem_limit_bytes=VMEM_LIMIT),
        name="router",
    )(h1, router_w, router_b)


def _dispatch_kernel(tail_ref, pos_ref, h_ref, xs_ref, zbuf, sem, zsem):
    tt = h_ref.shape[0]
    n_exp = tail_ref.shape[0]

    @pl.when(pl.program_id(0) == 0)
    def _():
        zbuf[...] = jnp.zeros_like(zbuf)

        def tail_copy(e):
            start = pl.multiple_of(tail_ref[e], zbuf.shape[0])
            return pltpu.make_async_copy(zbuf, xs_ref.at[pl.ds(start, zbuf.shape[0]), :], zsem)

        def zstart(e, carry):
            tail_copy(e).start()
            return carry

        def zwait(e, carry):
            tail_copy(e).wait()
            return carry

        lax.fori_loop(0, n_exp, zstart, 0)
        lax.fori_loop(0, n_exp, zwait, 0)

    def issue(t, carry):
        for kk in range(TOP_K):
            p = pos_ref[t * SLOTS + kk]
            pltpu.make_async_copy(h_ref.at[pl.ds(t, 1), :], xs_ref.at[pl.ds(p, 1), :], sem).start(priority=kk % 2)
        return carry

    lax.fori_loop(0, tt, issue, 0)
    for kk in range(TOP_K):
        pltpu.make_async_copy(h_ref, xs_ref.at[pl.ds(0, tt), :], sem).wait()


def _dispatch(tails, pos_flat, h1, rows_total):
    tokens, d_model = h1.shape
    tt = DISPATCH_TILE
    assert tokens % tt == 0
    grid_spec = pltpu.PrefetchScalarGridSpec(
        num_scalar_prefetch=1,
        grid=(tokens // tt,),
        in_specs=[pl.BlockSpec((tt * SLOTS,), lambda t, tl: (t,), memory_space=pltpu.SMEM),
                  pl.BlockSpec((tt, d_model), lambda t, tl: (t, 0))],
        out_specs=pl.BlockSpec(memory_space=pl.ANY),
        scratch_shapes=[pltpu.VMEM((EXPERT_ROWS, d_model), F32), pltpu.SemaphoreType.DMA,
                        pltpu.SemaphoreType.DMA],
    )
    return pl.pallas_call(
        _dispatch_kernel,
        grid_spec=grid_spec,
        out_shape=jax.ShapeDtypeStruct((rows_total, d_model), F32),
        compiler_params=pltpu.CompilerParams(dimension_semantics=("arbitrary",), vmem_limit_bytes=VMEM_LIMIT),
        name="dispatch",
    )(tails, pos_flat, h1)


def _expert_kernel(be_ref, na_ref, xs_ref, wgu_ref, wd_ref, ys_ref):
    del be_ref
    j = pl.program_id(0)
    d_exp = wd_ref.shape[0]

    @pl.when(j < na_ref[0])
    def _():
        gu = _dot(xs_ref[...].astype(BF16), wgu_ref[...])
        act = _silu(gu[:, :d_exp]) * gu[:, d_exp:]
        ys_ref[...] = _dot(act.astype(BF16), wd_ref[...])


def _experts(block_expert, n_active, xs, w_gu, w_down):
    rows_total, d_model = xs.shape
    eb = EXPERT_ROWS
    n_blocks = rows_total // eb
    d_exp = w_down.shape[1]

    def row_map(j, be, na):
        return (jnp.minimum(j, na[0] - 1), 0)

    def w_map(j, be, na):
        return (be[j], 0, 0)

    grid_spec = pltpu.PrefetchScalarGridSpec(
        num_scalar_prefetch=2,
        grid=(n_blocks,),
        in_specs=[pl.BlockSpec((eb, d_model), row_map),
                  pl.BlockSpec((None, d_model, 2 * d_exp), w_map),
                  pl.BlockSpec((None, d_exp, d_model), w_map)],
        out_specs=pl.BlockSpec((eb, d_model), row_map),
    )
    return pl.pallas_call(
        _expert_kernel,
        grid_spec=grid_spec,
        out_shape=jax.ShapeDtypeStruct((rows_total, d_model), F32),
        compiler_params=pltpu.CompilerParams(dimension_semantics=("arbitrary",), vmem_limit_bytes=VMEM_LIMIT),
        name="experts",
    )(block_expert, n_active, xs, w_gu, w_down)


def _combine_kernel(pos_ref, h_ref, w_ref, wsgu_ref, wsd_ref, l2g_ref, l2b_ref, ys_ref, out_ref, gbuf, sem):
    tt = h_ref.shape[0]
    d_sh = wsd_ref.shape[0]

    def issue(t, carry):
        for kk in range(TOP_K):
            p = pos_ref[t * SLOTS + kk]
            pltpu.make_async_copy(ys_ref.at[pl.ds(p, 1), :], gbuf.at[kk, pl.ds(t, 1), :], sem).start(
                priority=kk % 2)
        return carry

    lax.fori_loop(0, tt, issue, 0)

    h = h_ref[...]
    gu = _dot(h.astype(BF16), wsgu_ref[...])
    act = _silu(gu[:, :d_sh]) * gu[:, d_sh:]
    acc = DEEPNORM_ALPHA * h + _dot(act.astype(BF16), wsd_ref[...])

    for kk in range(TOP_K):
        pltpu.make_async_copy(ys_ref.at[pl.ds(0, tt), :], gbuf.at[kk], sem).wait()

    w = w_ref[...]
    for kk in range(TOP_K):
        acc = acc + w[:, kk:kk + 1] * gbuf[kk]
    out_ref[...] = _layer_norm(acc, l2g_ref[...], l2b_ref[...])


def _combine(pos_flat, h1, w_tab, ws_gu, ws_down, l2g, l2b, ys):
    tokens, d_model = h1.shape
    tt = COMBINE_TILE
    assert tokens % tt == 0
    full = lambda a: pl.BlockSpec(a.shape, lambda t: (0,) * a.ndim)
    return pl.pallas_call(
        _combine_kernel,
        grid=(tokens // tt,),
        in_specs=[pl.BlockSpec((tt * SLOTS,), lambda t: (t,), memory_space=pltpu.SMEM),
                  pl.BlockSpec((tt, d_model), lambda t: (t, 0)),
                  pl.BlockSpec((tt, SLOTS), lambda t: (t, 0)),
                  full(ws_gu), full(ws_down), full(l2g), full(l2b),
                  pl.BlockSpec(memory_space=pl.ANY)],
        out_specs=pl.BlockSpec((tt, d_model), lambda t: (t, 0)),
        out_shape=jax.ShapeDtypeStruct((tokens, d_model), F32),
        scratch_shapes=[pltpu.VMEM((TOP_K, tt, d_model), F32), pltpu.SemaphoreType.DMA],
        compiler_params=pltpu.CompilerParams(dimension_semantics=("arbitrary",), vmem_limit_bytes=VMEM_LIMIT),
        name="combine",
    )(pos_flat, h1, w_tab, ws_gu, ws_down, l2g, l2b, ys)


def kernel(x, meta_tokens, ln_in_g, ln_in_b, w_in, b_in, hg_lb_logits, hg_norm_g, conv_w, conv_b, lru_wa, lru_ba,
           lru_wx, lru_bx, lru_lambda, w_out, ln1_g, ln1_b, router_w, router_bias, we_gate, we_up, we_down,
           ws_gate, ws_up, ws_down, ln2_g, ln2_b):
    batch, seq, d_model = x.shape
    assert w_in.shape[0] == DEPTH
    n_exp = router_w.shape[-1]
    row = lambda a: a.reshape(1, -1).astype(F32)

    meta_pad = jnp.concatenate([jnp.zeros((ROW_BLOCK - N_META, d_model), F32), meta_tokens.astype(F32)], axis=0)
    w_lru = jnp.concatenate([lru_wa[0], lru_wx[0]], axis=-1).astype(BF16)
    h1 = _mixer(x, meta_pad, row(ln_in_g), row(ln_in_b), w_in[0].astype(BF16), row(b_in[0]),
                hg_lb_logits.astype(F32), row(hg_norm_g[0]), conv_w[0].astype(F32), row(conv_b[0]), w_lru,
                row(lru_ba[0]), row(lru_bx[0]), row(lru_lambda[0]), w_out[0].astype(BF16),
                row(ln1_g[0]), row(ln1_b[0]))
    tokens = batch * seq
    h1 = h1.reshape(tokens, d_model)

    e_tab, r_tab, w_tab, counts = _router(h1, router_w[0].astype(F32), row(router_bias[0]))

    eb = EXPERT_ROWS
    n_blocks = tokens * TOP_K // eb + n_exp
    counts = counts.reshape(n_exp).astype(jnp.int32)
    padded = (counts + eb - 1) // eb * eb
    pad_ends = jnp.cumsum(padded)
    pad_starts = pad_ends - padded
    n_active = (pad_ends[-1] // eb).astype(jnp.int32)
    blk = jnp.minimum(jnp.arange(n_blocks, dtype=jnp.int32), n_active - 1)
    block_expert = jnp.sum((pad_ends[None, :] <= (blk * eb)[:, None]).astype(jnp.int32), axis=1)
    block_expert = jnp.minimum(block_expert, n_exp - 1)
    pos = pad_starts[e_tab] + r_tab
    pos_flat = pos.reshape(tokens * SLOTS).astype(jnp.int32)
    tails = jnp.maximum(pad_ends - eb, 0).astype(jnp.int32)

    xs = _dispatch(tails, pos_flat, h1, n_blocks * eb)
    w_gu = jnp.concatenate([we_gate[0], we_up[0]], axis=-1).astype(BF16)
    ys = _experts(block_expert, n_active.reshape(1), xs, w_gu, we_down[0].astype(BF16))
    ws_gu = jnp.concatenate([ws_gate[0], ws_up[0]], axis=-1).astype(BF16)
    out = _combine(pos_flat, h1, w_tab, ws_gu, ws_down[0].astype(BF16), row(ln2_g[0]), row(ln2_b[0]), ys)
    return out.reshape(batch, seq, d_model)
```

```python
import functools

import jax
import jax.numpy as jnp
from jax import lax
from jax.experimental import pallas as pl
from jax.experimental.pallas import tpu as pltpu

F32 = jnp.float32
BF16 = jnp.bfloat16

N_META = 16
LN_EPS = 1e-5
RMS_EPS = 1e-6
HG_HEADS = 8
HEAD_DIM = 128
LRU_BLOCKS = 8
LRU_BLOCK = 128
CONV_W = 4
LRU_C = 8.0
TOP_K = 6
ROUTED_SCALE = 2.5
DEPTH = 1
DEEPNORM_ALPHA = (2.0 * DEPTH) ** 0.25
LOG2_E = 1.4426950408889634

ROW_BLOCK = 256
CHUNK = 128
ROUTER_TILE = 512
DISPATCH_TILE = 512
COMBINE_TILE = 256
EXPERT_ROWS = 512
SLOTS = 8
SUBLANES = 8
VMEM_LIMIT = 56 * 1024 * 1024


def _layer_norm(x, g, b):
    mu = jnp.mean(x, axis=-1, keepdims=True)
    xc = x - mu
    var = jnp.mean(xc * xc, axis=-1, keepdims=True)
    return xc * lax.rsqrt(var + LN_EPS) * g + b


def _sigmoid(x):
    return 1.0 / (1.0 + jnp.exp2(x * (-LOG2_E)))


def _silu(x):
    return x * _sigmoid(x)


def _gelu_tanh(x):
    return 0.5 * x * (1.0 + jnp.tanh(0.7978845608028654 * (x + 0.044715 * (x * x * x))))


def _dot(a, b):
    return jnp.dot(a, b, preferred_element_type=F32)


def _dot_nt(a, b):
    return lax.dot_general(a, b, (((1,), (1,)), ((), ())), preferred_element_type=F32)


def _split3(x):
    hi = x.astype(BF16)
    r1 = x - hi.astype(F32)
    mid = r1.astype(BF16)
    lo = (r1 - mid.astype(F32)).astype(BF16)
    return hi, mid, lo


def _linear_scan(a, u, h0):
    rows, width = a.shape
    groups = rows // SUBLANES
    a3 = a.reshape(groups, SUBLANES, width)
    u3 = u.reshape(groups, SUBLANES, width)
    sid = lax.broadcasted_iota(jnp.int32, (groups, SUBLANES, width), 1)
    d = 1
    while d < SUBLANES:
        keep = sid >= d
        a_sh = jnp.where(keep, pltpu.roll(a3, d, 1), 1.0)
        u_sh = jnp.where(keep, pltpu.roll(u3, d, 1), 0.0)
        u3 = u3 + a3 * u_sh
        a3 = a3 * a_sh
        d *= 2
    out = []
    carry = h0
    for g in range(groups):
        hg = u3[g] + a3[g] * carry
        out.append(hg)
        carry = hg[SUBLANES - 1:SUBLANES, :]
    return jnp.concatenate(out, axis=0)


def _mixer_kernel(x_ref, meta_ref, lng_ref, lnb_ref, win_ref, bin_ref, lbl_ref, hgn_ref, cw_ref, cb_ref,
                  wlru_ref, ba_ref, bx_ref, lam_ref, wout_ref, l1g_ref, l1b_ref,
                  h1_ref,
                  st_ref, cwin_ref, hcar_ref, yhg_ref, st0_ref, cwin0_ref, hcar0_ref):
    bi = pl.program_id(0)
    i = pl.program_id(1)
    is_first = i == 0

    @pl.when(jnp.logical_and(is_first, bi == 0))
    def _():
        st_ref[...] = jnp.zeros_like(st_ref)
        cwin_ref[0:SUBLANES, :] = jnp.zeros_like(cwin0_ref)
        hcar_ref[...] = jnp.zeros_like(hcar_ref)

    @pl.when(jnp.logical_and(is_first, bi > 0))
    def _():
        st_ref[...] = st0_ref[...]
        cwin_ref[0:SUBLANES, :] = cwin0_ref[...]
        hcar_ref[...] = hcar0_ref[...]

    @pl.when(jnp.logical_or(jnp.logical_not(is_first), bi == 0))
    def _():
        _mixer_block(x_ref, meta_ref, lng_ref, lnb_ref, win_ref, bin_ref, lbl_ref, hgn_ref, cw_ref, cb_ref,
                     wlru_ref, ba_ref, bx_ref, lam_ref, wout_ref, l1g_ref, l1b_ref, h1_ref,
                     st_ref, cwin_ref, hcar_ref, yhg_ref, is_first)

        @pl.when(is_first)
        def _():
            st0_ref[...] = st_ref[...]
            cwin0_ref[...] = cwin_ref[0:SUBLANES, :]
            hcar0_ref[...] = hcar_ref[...]


def _mixer_block(x_ref, meta_ref, lng_ref, lnb_ref, win_ref, bin_ref, lbl_ref, hgn_ref, cw_ref, cb_ref,
                 wlru_ref, ba_ref, bx_ref, lam_ref, wout_ref, l1g_ref, l1b_ref, h1_ref,
                 st_ref, cwin_ref, hcar_ref, yhg_ref, is_first):
    rows = x_ref.shape[0]
    d_model = x_ref.shape[1]
    rid = lax.broadcasted_iota(jnp.int32, (rows, 1), 0)
    valid = jnp.logical_or(rid >= rows - N_META, jnp.logical_not(is_first))
    xin = jnp.where(is_first, meta_ref[...], x_ref[...])
    h = _layer_norm(xin, lng_ref[...], lnb_ref[...])
    hb = h.astype(BF16)

    def proj(seg):
        cols = slice(seg * d_model, (seg + 1) * d_model)
        return _dot(hb, win_ref[:, cols]) + bin_ref[:, cols]

    lbl = lbl_ref[...]
    lmax = jnp.max(lbl, axis=0, keepdims=True)
    lexp = jnp.exp(lbl - lmax)
    lb = lexp[0:1, :] / jnp.sum(lexp, axis=0, keepdims=True)

    q = _silu(proj(0))
    f = lb + (1.0 - lb) * _sigmoid(proj(1))
    lf = jnp.where(valid, jnp.log(f), 0.0)
    k = jnp.where(valid, 1.0 - f, 0.0)
    v = proj(2)

    r_i = lax.broadcasted_iota(jnp.int32, (CHUNK, CHUNK), 0)
    c_i = lax.broadcasted_iota(jnp.int32, (CHUNK, CHUNK), 1)
    causal = r_i >= c_i
    tri = jnp.where(causal, 1.0, 0.0).astype(BF16)
    hgn = hgn_ref[...]
    half = CHUNK // 2
    for c in range(rows // CHUNK):
        rs = slice(c * CHUNK, (c + 1) * CHUNK)
        lf_hi, lf_mid, lf_lo = _split3(lf[rs])
        b = _dot(tri, lf_hi) + _dot(tri, lf_mid) + _dot(tri, lf_lo)
        b_mid = b[half - 1:half, :]
        b_last = b[CHUNK - 1:CHUNK, :]
        q_t = q[rs] * jnp.exp(b - b_mid)
        k_t = k[rs] * jnp.exp(b_mid - b)
        q_s = (q_t * jnp.exp(b_mid)).astype(BF16)
        k_s = (k_t * jnp.exp(b_last - b_mid)).astype(BF16)
        q_t = q_t.astype(BF16)
        k_t = k_t.astype(BF16)
        s_decay = jnp.exp(b_last)
        vc = v[rs]
        vb = vc.astype(BF16)
        for hd in range(HG_HEADS):
            cs = slice(hd * HEAD_DIM, (hd + 1) * HEAD_DIM)
            p = jnp.where(causal, _dot_nt(q_t[:, cs], k_t[:, cs]), 0.0)
            st = st_ref[hd]
            o = _dot(p.astype(BF16), vb[:, cs]) + _dot_nt(q_s[:, cs], st.astype(BF16))
            st_ref[hd] = st * s_decay[:, cs] + _dot(vc[:, cs].T.astype(BF16), k_s[:, cs])
            o = o * lax.rsqrt(jnp.mean(o * o, axis=-1, keepdims=True) + RMS_EPS) * hgn
            yhg_ref[rs, cs] = o
    y_hg = yhg_ref[...] * _silu(proj(3))

    lx = jnp.where(valid, proj(4), 0.0)
    cwin_ref[SUBLANES:SUBLANES + rows, :] = lx
    cw = cw_ref[...]
    first_tap = SUBLANES - (CONV_W - 1)
    xc = cb_ref[...] + cw[0:1, :] * cwin_ref[first_tap:first_tap + rows, :]
    for j in range(1, CONV_W):
        xc = xc + cw[j:j + 1, :] * cwin_ref[first_tap + j:first_tap + j + rows, :]
    cwin_ref[0:SUBLANES, :] = lx[rows - SUBLANES:rows, :]

    ra_parts, ri_parts = [], []
    for n in range(LRU_BLOCKS):
        cs = slice(n * LRU_BLOCK, (n + 1) * LRU_BLOCK)
        g2 = _dot(xc[:, cs].astype(BF16), wlru_ref[n])
        ra_parts.append(g2[:, :LRU_BLOCK])
        ri_parts.append(g2[:, LRU_BLOCK:])
    r_gate = _sigmoid(jnp.concatenate(ra_parts, axis=1) + ba_ref[...])
    i_gate = _sigmoid(jnp.concatenate(ri_parts, axis=1) + bx_ref[...])
    lam = lam_ref[...]
    neg = -lam
    softplus_neg = jnp.maximum(neg, 0.0) + jnp.log(1.0 + jnp.exp(-jnp.abs(neg)))
    log_a = -LRU_C * r_gate * softplus_neg
    a = jnp.exp(log_a)
    u = jnp.where(valid, jnp.sqrt(1.0 - a * a) * (i_gate * xc), 0.0)
    h_lru = _linear_scan(a, u, hcar_ref[0:1, :])
    hcar_ref[0:1, :] = h_lru[rows - 1:rows, :]
    y_lru = h_lru * _gelu_tanh(proj(5))

    merged = _sigmoid(proj(6)) * y_hg + _sigmoid(proj(7)) * y_lru
    mix = _dot(merged.astype(BF16), wout_ref[...])
    h1_ref[...] = _layer_norm(DEEPNORM_ALPHA * h + mix, l1g_ref[...], l1b_ref[...])


def _const_spec(shape):
    nd = len(shape)
    return pl.BlockSpec(shape, lambda b, i: (0,) * nd, pipeline_mode=pl.Buffered(1))


def _mixer(x, meta_pad, ln_g, ln_b, w_in, b_in, lb_logits, hg_norm, conv_w, conv_b, w_lru, ba, bx, lam,
           w_out, l1g, l1b):
    batch, seq, d_model = x.shape
    rows = ROW_BLOCK
    assert seq % rows == 0 and d_model == HG_HEADS * HEAD_DIM == LRU_BLOCKS * LRU_BLOCK
    n_blk = seq // rows + 1
    consts = (meta_pad, ln_g, ln_b, w_in, b_in, lb_logits, hg_norm, conv_w, conv_b, w_lru, ba, bx, lam,
              w_out, l1g, l1b)
    x_spec = pl.BlockSpec((None, rows, d_model), lambda b, i: (b, jnp.maximum(i - 1, 0), 0))
    return pl.pallas_call(
        _mixer_kernel,
        grid=(batch, n_blk),
        in_specs=[x_spec] + [_const_spec(c.shape) for c in consts],
        out_specs=pl.BlockSpec((None, rows, d_model), lambda b, i: (b, jnp.maximum(i - 1, 0), 0)),
        out_shape=jax.ShapeDtypeStruct((batch, seq, d_model), F32),
        scratch_shapes=[
            pltpu.VMEM((HG_HEADS, HEAD_DIM, HEAD_DIM), F32),
            pltpu.VMEM((rows + SUBLANES, d_model), F32),
            pltpu.VMEM((SUBLANES, d_model), F32),
            pltpu.VMEM((rows, d_model), F32),
            pltpu.VMEM((HG_HEADS, HEAD_DIM, HEAD_DIM), F32),
            pltpu.VMEM((SUBLANES, d_model), F32),
            pltpu.VMEM((SUBLANES, d_model), F32),
        ],
        compiler_params=pltpu.CompilerParams(
            dimension_semantics=("arbitrary", "arbitrary"), vmem_limit_bytes=VMEM_LIMIT),
        name="mixer",
    )(x, *consts)


def _router_kernel(h_ref, rw_ref, rb_ref, e_ref, r_ref, w_ref, cnt_ref):
    step = pl.program_id(0)

    @pl.when(step == 0)
    def _():
        cnt_ref[...] = jnp.zeros_like(cnt_ref)

    h = h_ref[...]
    tt = h.shape[0]
    n_exp = rw_ref.shape[1]
    h_hi = h.astype(BF16)
    h_lo = (h - h_hi.astype(F32)).astype(BF16)
    rw = rw_ref[...]
    w_hi = rw.astype(BF16)
    w_lo = (rw - w_hi.astype(F32)).astype(BF16)
    logits = _dot(h_hi, w_hi) + _dot(h_hi, w_lo) + _dot(h_lo, w_hi)
    scores = _sigmoid(logits)
    sel = scores + rb_ref[...]
    lane = lax.broadcasted_iota(jnp.int32, (tt, n_exp), 1)
    onehots = []
    for _ in range(TOP_K):
        m = jnp.max(sel, axis=-1, keepdims=True)
        idx = jnp.min(jnp.where(sel == m, lane, n_exp), axis=-1, keepdims=True)
        oh = lane == idx
        onehots.append(oh)
        sel = jnp.where(oh, -jnp.inf, sel)
    chosen = onehots[0]
    for oh in onehots[1:]:
        chosen = jnp.logical_or(chosen, oh)
    chosen_f = jnp.where(chosen, 1.0, 0.0)
    r_i = lax.broadcasted_iota(jnp.int32, (tt, tt), 0)
    c_i = lax.broadcasted_iota(jnp.int32, (tt, tt), 1)
    strict = jnp.where(r_i > c_i, 1.0, 0.0).astype(BF16)
    counts = cnt_ref[...]
    rank = _dot(strict, chosen_f.astype(BF16)) + counts
    cnt_ref[...] = counts + jnp.sum(chosen_f, axis=0, keepdims=True)

    slot = lax.broadcasted_iota(jnp.int32, (tt, SLOTS), 1)
    s_cols = [jnp.sum(jnp.where(oh, scores, 0.0), axis=-1, keepdims=True) for oh in onehots]
    denom = s_cols[0]
    for s in s_cols[1:]:
        denom = denom + s
    e_tab = jnp.zeros((tt, SLOTS), jnp.int32)
    r_tab = jnp.zeros((tt, SLOTS), jnp.int32)
    w_tab = jnp.zeros((tt, SLOTS), F32)
    for kk, oh in enumerate(onehots):
        e_col = jnp.sum(jnp.where(oh, lane, 0), axis=-1, keepdims=True)
        r_col = jnp.sum(jnp.where(oh, rank, 0.0), axis=-1, keepdims=True).astype(jnp.int32)
        w_col = s_cols[kk] / denom * ROUTED_SCALE
        e_tab = jnp.where(slot == kk, e_col, e_tab)
        r_tab = jnp.where(slot == kk, r_col, r_tab)
        w_tab = jnp.where(slot == kk, w_col, w_tab)
    e_ref[...] = e_tab
    r_ref[...] = r_tab
    w_ref[...] = w_tab


def _router(h1, router_w, router_b):
    tokens, d_model = h1.shape
    n_exp = router_w.shape[1]
    tt = ROUTER_TILE
    assert tokens % tt == 0
    tab = lambda dt: jax.ShapeDtypeStruct((tokens, SLOTS), dt)
    tab_spec = pl.BlockSpec((tt, SLOTS), lambda t: (t, 0))
    return pl.pallas_call(
        _router_kernel,
        grid=(tokens // tt,),
        in_specs=[pl.BlockSpec((tt, d_model), lambda t: (t, 0)),
                  pl.BlockSpec((d_model, n_exp), lambda t: (0, 0)),
                  pl.BlockSpec((1, n_exp), lambda t: (0, 0))],
        out_specs=[tab_spec, tab_spec, tab_spec, pl.BlockSpec((1, n_exp), lambda t: (0, 0))],
        out_shape=[tab(jnp.int32), tab(jnp.int32), tab(F32), jax.ShapeDtypeStruct((1, n_exp), F32)],
        compiler_params=pltpu.CompilerParams(dimension_semantics=("arbitrary",), vmem_limit_bytes=VMEM_LIMIT),
        name="router",
    )(h1, router_w, router_b)


def _dispatch_kernel(tail_ref, pos_ref, h_ref, xs_ref, zbuf, sem, zsem):
    tt = h_ref.shape[0]
    n_exp = tail_ref.shape[0]

    @pl.when(pl.program_id(0) == 0)
    def _():
        zbuf[...] = jnp.zeros_like(zbuf)

        def tail_copy(e):
            start = pl.multiple_of(tail_ref[e], zbuf.shape[0])
            return pltpu.make_async_copy(zbuf, xs_ref.at[pl.ds(start, zbuf.shape[0]), :], zsem)

        def zstart(e, carry):
            tail_copy(e).start()
            return carry

        def zwait(e, carry):
            tail_copy(e).wait()
            return carry

        lax.fori_loop(0, n_exp, zstart, 0)
        lax.fori_loop(0, n_exp, zwait, 0)

    def issue(t, carry):
        for kk in range(TOP_K):
            p = pos_ref[t * SLOTS + kk]
            pltpu.make_async_copy(h_ref.at[pl.ds(t, 1), :], xs_ref.at[pl.ds(p, 1), :], sem).start(priority=kk % 2)
        return carry

    lax.fori_loop(0, tt, issue, 0)
    for kk in range(TOP_K):
        pltpu.make_async_copy(h_ref, xs_ref.at[pl.ds(0, tt), :], sem).wait()


def _dispatch(tails, pos_flat, h1, rows_total):
    tokens, d_model = h1.shape
    tt = DISPATCH_TILE
    assert tokens % tt == 0
    grid_spec = pltpu.PrefetchScalarGridSpec(
        num_scalar_prefetch=1,
        grid=(tokens // tt,),
        in_specs=[pl.BlockSpec((tt * SLOTS,), lambda t, tl: (t,), memory_space=pltpu.SMEM),
                  pl.BlockSpec((tt, d_model), lambda t, tl: (t, 0))],
        out_specs=pl.BlockSpec(memory_space=pl.ANY),
        scratch_shapes=[pltpu.VMEM((EXPERT_ROWS, d_model), F32),
                        pltpu.SemaphoreType.DMA, pltpu.SemaphoreType.DMA],
    )
    return pl.pallas_call(
        _dispatch_kernel,
        grid_spec=grid_spec,
        out_shape=jax.ShapeDtypeStruct((rows_total, d_model), F32),
        compiler_params=pltpu.CompilerParams(dimension_semantics=("arbitrary",), vmem_limit_bytes=VMEM_LIMIT),
        name="dispatch",
    )(tails, pos_flat, h1)


def _expert_kernel(be_ref, na_ref, xs_ref, wgu_ref, wd_ref, ys_ref):
    del be_ref
    j = pl.program_id(0)
    d_exp = wd_ref.shape[0]

    @pl.when(j < na_ref[0])
    def _():
        gu = _dot(xs_ref[...].astype(BF16), wgu_ref[...])
        act = _silu(gu[:, :d_exp]) * gu[:, d_exp:]
        ys_ref[...] = _dot(act.astype(BF16), wd_ref[...])


def _experts(block_expert, n_active, xs, w_gu, w_down):
    rows_total, d_model = xs.shape
    eb = EXPERT_ROWS
    n_blocks = rows_total // eb
    d_exp = w_down.shape[1]

    def row_map(j, be, na):
        return (jnp.minimum(j, na[0] - 1), 0)

    def w_map(j, be, na):
        return (be[j], 0, 0)

    grid_spec = pltpu.PrefetchScalarGridSpec(
        num_scalar_prefetch=2,
        grid=(n_blocks,),
        in_specs=[pl.BlockSpec((eb, d_model), row_map),
                  pl.BlockSpec((None, d_model, 2 * d_exp), w_map),
                  pl.BlockSpec((None, d_exp, d_model), w_map)],
        out_specs=pl.BlockSpec((eb, d_model), row_map),
    )
    return pl.pallas_call(
        _expert_kernel,
        grid_spec=grid_spec,
        out_shape=jax.ShapeDtypeStruct((rows_total, d_model), F32),
        compiler_params=pltpu.CompilerParams(dimension_semantics=("arbitrary",), vmem_limit_bytes=VMEM_LIMIT),
        name="experts",
    )(block_expert, n_active, xs, w_gu, w_down)


def _combine_kernel(pos_ref, nxt_ref, h_ref, w_ref, wsgu_ref, wsd_ref, l2g_ref, l2b_ref, ys_ref, out_ref,
                    gbuf, sems):
    step = pl.program_id(0)
    n_steps = pl.num_programs(0)
    tt = h_ref.shape[0]
    d_sh = wsd_ref.shape[0]
    cur = step % 2

    def gather(table_ref, buf):
        def issue(t, carry):
            for kk in range(TOP_K):
                p = table_ref[t * SLOTS + kk]
                pltpu.make_async_copy(ys_ref.at[pl.ds(p, 1), :], gbuf.at[buf, kk, pl.ds(t, 1), :],
                                      sems.at[buf]).start(priority=kk % 2)
            return carry

        lax.fori_loop(0, tt, issue, 0)

    @pl.when(step == 0)
    def _():
        gather(pos_ref, 0)

    @pl.when(step + 1 < n_steps)
    def _():
        gather(nxt_ref, 1 - cur)

    h = h_ref[...]
    gu = _dot(h.astype(BF16), wsgu_ref[...])
    act = _silu(gu[:, :d_sh]) * gu[:, d_sh:]
    acc = DEEPNORM_ALPHA * h + _dot(act.astype(BF16), wsd_ref[...])

    for kk in range(TOP_K):
        pltpu.make_async_copy(ys_ref.at[pl.ds(0, tt), :], gbuf.at[cur, kk], sems.at[cur]).wait()

    w = w_ref[...]
    for kk in range(TOP_K):
        acc = acc + w[:, kk:kk + 1] * gbuf[cur, kk]
    out_ref[...] = _layer_norm(acc, l2g_ref[...], l2b_ref[...])


def _combine(pos_flat, h1, w_tab, ws_gu, ws_down, l2g, l2b, ys):
    tokens, d_model = h1.shape
    tt = COMBINE_TILE
    assert tokens % tt == 0
    full = lambda a: pl.BlockSpec(a.shape, lambda t: (0,) * a.ndim)
    n_tiles = tokens // tt
    return pl.pallas_call(
        _combine_kernel,
        grid=(n_tiles,),
        in_specs=[pl.BlockSpec((tt * SLOTS,), lambda t: (t,), memory_space=pltpu.SMEM),
                  pl.BlockSpec((tt * SLOTS,), lambda t: (jnp.minimum(t + 1, n_tiles - 1),),
                               memory_space=pltpu.SMEM),
                  pl.BlockSpec((tt, d_model), lambda t: (t, 0)),
                  pl.BlockSpec((tt, SLOTS), lambda t: (t, 0)),
                  full(ws_gu), full(ws_down), full(l2g), full(l2b),
                  pl.BlockSpec(memory_space=pl.ANY)],
        out_specs=pl.BlockSpec((tt, d_model), lambda t: (t, 0)),
        out_shape=jax.ShapeDtypeStruct((tokens, d_model), F32),
        scratch_shapes=[pltpu.VMEM((2, TOP_K, tt, d_model), F32), pltpu.SemaphoreType.DMA((2,))],
        compiler_params=pltpu.CompilerParams(dimension_semantics=("arbitrary",), vmem_limit_bytes=VMEM_LIMIT),
        name="combine",
    )(pos_flat, pos_flat, h1, w_tab, ws_gu, ws_down, l2g, l2b, ys)


def kernel(x, meta_tokens, ln_in_g, ln_in_b, w_in, b_in, hg_lb_logits, hg_norm_g, conv_w, conv_b, lru_wa, lru_ba,
           lru_wx, lru_bx, lru_lambda, w_out, ln1_g, ln1_b, router_w, router_bias, we_gate, we_up, we_down,
           ws_gate, ws_up, ws_down, ln2_g, ln2_b):
    batch, seq, d_model = x.shape
    assert w_in.shape[0] == DEPTH
    n_exp = router_w.shape[-1]
    row = lambda a: a.reshape(1, -1).astype(F32)

    meta_pad = jnp.concatenate([jnp.zeros((ROW_BLOCK - N_META, d_model), F32), meta_tokens.astype(F32)], axis=0)
    w_lru = jnp.concatenate([lru_wa[0], lru_wx[0]], axis=-1).astype(BF16)
    h1 = _mixer(x, meta_pad, row(ln_in_g), row(ln_in_b), w_in[0].astype(BF16), row(b_in[0]),
                hg_lb_logits.astype(F32), row(hg_norm_g[0]), conv_w[0].astype(F32), row(conv_b[0]), w_lru,
                row(lru_ba[0]), row(lru_bx[0]), row(lru_lambda[0]), w_out[0].astype(BF16),
                row(ln1_g[0]), row(ln1_b[0]))
    tokens = batch * seq
    h1 = h1.reshape(tokens, d_model)

    e_tab, r_tab, w_tab, counts = _router(h1, router_w[0].astype(F32), row(router_bias[0]))

    eb = EXPERT_ROWS
    n_blocks = tokens * TOP_K // eb + n_exp
    counts = counts.reshape(n_exp).astype(jnp.int32)
    padded = (counts + eb - 1) // eb * eb
    pad_ends = jnp.cumsum(padded)
    pad_starts = pad_ends - padded
    n_active = (pad_ends[-1] // eb).astype(jnp.int32)
    blk = jnp.minimum(jnp.arange(n_blocks, dtype=jnp.int32), n_active - 1)
    block_expert = jnp.sum((pad_ends[None, :] <= (blk * eb)[:, None]).astype(jnp.int32), axis=1)
    block_expert = jnp.minimum(block_expert, n_exp - 1)
    pos = pad_starts[e_tab] + r_tab
    pos_flat = pos.reshape(tokens * SLOTS).astype(jnp.int32)
    tails = jnp.maximum(pad_ends - eb, 0).astype(jnp.int32)

    xs = _dispatch(tails, pos_flat, h1, n_blocks * eb)
    w_gu = jnp.concatenate([we_gate[0], we_up[0]], axis=-1).astype(BF16)
    ys = _experts(block_expert, n_active.reshape(1), xs, w_gu, we_down[0].astype(BF16))
    ws_gu = jnp.concatenate([ws_gate[0], ws_up[0]], axis=-1).astype(BF16)
    out = _combine(pos_flat, h1, w_tab, ws_gu, ws_down[0].astype(BF16), row(ln2_g[0]), row(ln2_b[0]), ys)
    return out.reshape(batch, seq, d_model)
```

```python
import functools

import jax
import jax.numpy as jnp
from jax import lax
from jax.experimental import pallas as pl
from jax.experimental.pallas import tpu as pltpu

F32 = jnp.float32
BF16 = jnp.bfloat16

N_META = 16
LN_EPS = 1e-5
RMS_EPS = 1e-6
HG_HEADS = 8
HEAD_DIM = 128
LRU_BLOCKS = 8
LRU_BLOCK = 128
CONV_W = 4
LRU_C = 8.0
TOP_K = 6
ROUTED_SCALE = 2.5
DEPTH = 1
DEEPNORM_ALPHA = (2.0 * DEPTH) ** 0.25
LOG2_E = 1.4426950408889634

ROW_BLOCK = 256
CHUNK = 128
ROUTER_TILE = 512
DISPATCH_TILE = 512
COMBINE_TILE = 256
EXPERT_ROWS = 512
SLOTS = 8
SUBLANES = 8
TAG_W = 128
SLOT_ROWS = 4096
ROW_UNROLL = 8
VMEM_LIMIT = 56 * 1024 * 1024


def _layer_norm(x, g, b):
    mu = jnp.mean(x, axis=-1, keepdims=True)
    xc = x - mu
    var = jnp.mean(xc * xc, axis=-1, keepdims=True)
    return xc * lax.rsqrt(var + LN_EPS) * g + b


def _sigmoid(x):
    return 1.0 / (1.0 + jnp.exp2(x * (-LOG2_E)))


def _silu(x):
    return x * _sigmoid(x)


def _gelu_tanh(x):
    return 0.5 * x * (1.0 + jnp.tanh(0.7978845608028654 * (x + 0.044715 * (x * x * x))))


def _dot(a, b):
    return jnp.dot(a, b, preferred_element_type=F32)


def _dot_nt(a, b):
    return lax.dot_general(a, b, (((1,), (1,)), ((), ())), preferred_element_type=F32)


def _split3(x):
    hi = x.astype(BF16)
    r1 = x - hi.astype(F32)
    mid = r1.astype(BF16)
    lo = (r1 - mid.astype(F32)).astype(BF16)
    return hi, mid, lo


def _linear_scan(a, u, h0):
    rows, width = a.shape
    groups = rows // SUBLANES
    a3 = a.reshape(groups, SUBLANES, width)
    u3 = u.reshape(groups, SUBLANES, width)
    sid = lax.broadcasted_iota(jnp.int32, (groups, SUBLANES, width), 1)
    d = 1
    while d < SUBLANES:
        keep = sid >= d
        a_sh = jnp.where(keep, pltpu.roll(a3, d, 1), 1.0)
        u_sh = jnp.where(keep, pltpu.roll(u3, d, 1), 0.0)
        u3 = u3 + a3 * u_sh
        a3 = a3 * a_sh
        d *= 2
    out = []
    carry = h0
    for g in range(groups):
        hg = u3[g] + a3[g] * carry
        out.append(hg)
        carry = hg[SUBLANES - 1:SUBLANES, :]
    return jnp.concatenate(out, axis=0)


def _mixer_kernel(x_ref, meta_ref, lng_ref, lnb_ref, win_ref, bin_ref, lbl_ref, hgn_ref, cw_ref, cb_ref,
                  wlru_ref, ba_ref, bx_ref, lam_ref, wout_ref, l1g_ref, l1b_ref,
                  h1_ref,
                  st_ref, cwin_ref, hcar_ref, yhg_ref, st0_ref, cwin0_ref, hcar0_ref):
    bi = pl.program_id(0)
    i = pl.program_id(1)
    is_first = i == 0

    @pl.when(jnp.logical_and(is_first, bi == 0))
    def _():
        st_ref[...] = jnp.zeros_like(st_ref)
        cwin_ref[0:SUBLANES, :] = jnp.zeros_like(cwin0_ref)
        hcar_ref[...] = jnp.zeros_like(hcar_ref)

    @pl.when(jnp.logical_and(is_first, bi > 0))
    def _():
        st_ref[...] = st0_ref[...]
        cwin_ref[0:SUBLANES, :] = cwin0_ref[...]
        hcar_ref[...] = hcar0_ref[...]

    @pl.when(jnp.logical_or(jnp.logical_not(is_first), bi == 0))
    def _():
        _mixer_block(x_ref, meta_ref, lng_ref, lnb_ref, win_ref, bin_ref, lbl_ref, hgn_ref, cw_ref, cb_ref,
                     wlru_ref, ba_ref, bx_ref, lam_ref, wout_ref, l1g_ref, l1b_ref, h1_ref,
                     st_ref, cwin_ref, hcar_ref, yhg_ref, is_first)

        @pl.when(is_first)
        def _():
            st0_ref[...] = st_ref[...]
            cwin0_ref[...] = cwin_ref[0:SUBLANES, :]
            hcar0_ref[...] = hcar_ref[...]


def _mixer_block(x_ref, meta_ref, lng_ref, lnb_ref, win_ref, bin_ref, lbl_ref, hgn_ref, cw_ref, cb_ref,
                 wlru_ref, ba_ref, bx_ref, lam_ref, wout_ref, l1g_ref, l1b_ref, h1_ref,
                 st_ref, cwin_ref, hcar_ref, yhg_ref, is_first):
    rows = x_ref.shape[0]
    d_model = x_ref.shape[1]
    rid = lax.broadcasted_iota(jnp.int32, (rows, 1), 0)
    valid = jnp.logical_or(rid >= rows - N_META, jnp.logical_not(is_first))
    xin = jnp.where(is_first, meta_ref[...], x_ref[...])
    h = _layer_norm(xin, lng_ref[...], lnb_ref[...])
    hb = h.astype(BF16)

    def proj(seg):
        cols = slice(seg * d_model, (seg + 1) * d_model)
        return _dot(hb, win_ref[:, cols]) + bin_ref[:, cols]

    lbl = lbl_ref[...]
    lmax = jnp.max(lbl, axis=0, keepdims=True)
    lexp = jnp.exp(lbl - lmax)
    lb = lexp[0:1, :] / jnp.sum(lexp, axis=0, keepdims=True)

    q = _silu(proj(0))
    f = lb + (1.0 - lb) * _sigmoid(proj(1))
    lf = jnp.where(valid, jnp.log(f), 0.0)
    k = jnp.where(valid, 1.0 - f, 0.0)
    v = proj(2)

    r_i = lax.broadcasted_iota(jnp.int32, (CHUNK, CHUNK), 0)
    c_i = lax.broadcasted_iota(jnp.int32, (CHUNK, CHUNK), 1)
    causal = r_i >= c_i
    tri = jnp.where(causal, 1.0, 0.0).astype(BF16)
    hgn = hgn_ref[...]
    half = CHUNK // 2
    for c in range(rows // CHUNK):
        rs = slice(c * CHUNK, (c + 1) * CHUNK)
        lf_hi, lf_mid, lf_lo = _split3(lf[rs])
        b = _dot(tri, lf_hi) + _dot(tri, lf_mid) + _dot(tri, lf_lo)
        b_mid = b[half - 1:half, :]
        b_last = b[CHUNK - 1:CHUNK, :]
        q_t = q[rs] * jnp.exp(b - b_mid)
        k_t = k[rs] * jnp.exp(b_mid - b)
        q_s = (q_t * jnp.exp(b_mid)).astype(BF16)
        k_s = (k_t * jnp.exp(b_last - b_mid)).astype(BF16)
        q_t = q_t.astype(BF16)
        k_t = k_t.astype(BF16)
        s_decay = jnp.exp(b_last)
        vc = v[rs]
        vb = vc.astype(BF16)
        for hd in range(HG_HEADS):
            cs = slice(hd * HEAD_DIM, (hd + 1) * HEAD_DIM)
            p = jnp.where(causal, _dot_nt(q_t[:, cs], k_t[:, cs]), 0.0)
            st = st_ref[hd]
            o = _dot(p.astype(BF16), vb[:, cs]) + _dot_nt(q_s[:, cs], st.astype(BF16))
            st_ref[hd] = st * s_decay[:, cs] + _dot(vc[:, cs].T.astype(BF16), k_s[:, cs])
            o = o * lax.rsqrt(jnp.mean(o * o, axis=-1, keepdims=True) + RMS_EPS) * hgn
            yhg_ref[rs, cs] = o
    y_hg = yhg_ref[...] * _silu(proj(3))

    lx = jnp.where(valid, proj(4), 0.0)
    cwin_ref[SUBLANES:SUBLANES + rows, :] = lx
    cw = cw_ref[...]
    first_tap = SUBLANES - (CONV_W - 1)
    xc = cb_ref[...] + cw[0:1, :] * cwin_ref[first_tap:first_tap + rows, :]
    for j in range(1, CONV_W):
        xc = xc + cw[j:j + 1, :] * cwin_ref[first_tap + j:first_tap + j + rows, :]
    cwin_ref[0:SUBLANES, :] = lx[rows - SUBLANES:rows, :]

    ra_parts, ri_parts = [], []
    for n in range(LRU_BLOCKS):
        cs = slice(n * LRU_BLOCK, (n + 1) * LRU_BLOCK)
        g2 = _dot(xc[:, cs].astype(BF16), wlru_ref[n])
        ra_parts.append(g2[:, :LRU_BLOCK])
        ri_parts.append(g2[:, LRU_BLOCK:])
    r_gate = _sigmoid(jnp.concatenate(ra_parts, axis=1) + ba_ref[...])
    i_gate = _sigmoid(jnp.concatenate(ri_parts, axis=1) + bx_ref[...])
    lam = lam_ref[...]
    neg = -lam
    softplus_neg = jnp.maximum(neg, 0.0) + jnp.log(1.0 + jnp.exp(-jnp.abs(neg)))
    log_a = -LRU_C * r_gate * softplus_neg
    a = jnp.exp(log_a)
    u = jnp.where(valid, jnp.sqrt(1.0 - a * a) * (i_gate * xc), 0.0)
    h_lru = _linear_scan(a, u, hcar_ref[0:1, :])
    hcar_ref[0:1, :] = h_lru[rows - 1:rows, :]
    y_lru = h_lru * _gelu_tanh(proj(5))

    merged = _sigmoid(proj(6)) * y_hg + _sigmoid(proj(7)) * y_lru
    mix = _dot(merged.astype(BF16), wout_ref[...])
    h1_ref[...] = _layer_norm(DEEPNORM_ALPHA * h + mix, l1g_ref[...], l1b_ref[...])


def _const_spec(shape):
    nd = len(shape)
    return pl.BlockSpec(shape, lambda b, i: (0,) * nd, pipeline_mode=pl.Buffered(1))


def _mixer(x, meta_pad, ln_g, ln_b, w_in, b_in, lb_logits, hg_norm, conv_w, conv_b, w_lru, ba, bx, lam,
           w_out, l1g, l1b):
    batch, seq, d_model = x.shape
    rows = ROW_BLOCK
    assert seq % rows == 0 and d_model == HG_HEADS * HEAD_DIM == LRU_BLOCKS * LRU_BLOCK
    n_blk = seq // rows + 1
    consts = (meta_pad, ln_g, ln_b, w_in, b_in, lb_logits, hg_norm, conv_w, conv_b, w_lru, ba, bx, lam,
              w_out, l1g, l1b)
    x_spec = pl.BlockSpec((None, rows, d_model), lambda b, i: (b, jnp.maximum(i - 1, 0), 0))
    return pl.pallas_call(
        _mixer_kernel,
        grid=(batch, n_blk),
        in_specs=[x_spec] + [_const_spec(c.shape) for c in consts],
        out_specs=pl.BlockSpec((None, rows, d_model), lambda b, i: (b, jnp.maximum(i - 1, 0), 0)),
        out_shape=jax.ShapeDtypeStruct((batch, seq, d_model), F32),
        scratch_shapes=[
            pltpu.VMEM((HG_HEADS, HEAD_DIM, HEAD_DIM), F32),
            pltpu.VMEM((rows + SUBLANES, d_model), F32),
            pltpu.VMEM((SUBLANES, d_model), F32),
            pltpu.VMEM((rows, d_model), F32),
            pltpu.VMEM((HG_HEADS, HEAD_DIM, HEAD_DIM), F32),
            pltpu.VMEM((SUBLANES, d_model), F32),
            pltpu.VMEM((SUBLANES, d_model), F32),
        ],
        compiler_params=pltpu.CompilerParams(
            dimension_semantics=("arbitrary", "arbitrary"), vmem_limit_bytes=VMEM_LIMIT),
        name="mixer",
    )(x, *consts)


def _router_kernel(h_ref, rw_ref, rb_ref, e_ref, r_ref, w_ref, cnt_ref):
    step = pl.program_id(0)

    @pl.when(step == 0)
    def _():
        cnt_ref[...] = jnp.zeros_like(cnt_ref)

    h = h_ref[...]
    tt = h.shape[0]
    n_exp = rw_ref.shape[1]
    h_hi = h.astype(BF16)
    h_lo = (h - h_hi.astype(F32)).astype(BF16)
    rw = rw_ref[...]
    w_hi = rw.astype(BF16)
    w_lo = (rw - w_hi.astype(F32)).astype(BF16)
    logits = _dot(h_hi, w_hi) + _dot(h_hi, w_lo) + _dot(h_lo, w_hi)
    scores = _sigmoid(logits)
    sel = scores + rb_ref[...]
    lane = lax.broadcasted_iota(jnp.int32, (tt, n_exp), 1)
    onehots = []
    for _ in range(TOP_K):
        m = jnp.max(sel, axis=-1, keepdims=True)
        idx = jnp.min(jnp.where(sel == m, lane, n_exp), axis=-1, keepdims=True)
        oh = lane == idx
        onehots.append(oh)
        sel = jnp.where(oh, -jnp.inf, sel)
    chosen = onehots[0]
    for oh in onehots[1:]:
        chosen = jnp.logical_or(chosen, oh)
    chosen_f = jnp.where(chosen, 1.0, 0.0)
    r_i = lax.broadcasted_iota(jnp.int32, (tt, tt), 0)
    c_i = lax.broadcasted_iota(jnp.int32, (tt, tt), 1)
    strict = jnp.where(r_i > c_i, 1.0, 0.0).astype(BF16)
    counts = cnt_ref[...]
    rank = _dot(strict, chosen_f.astype(BF16)) + counts
    cnt_ref[...] = counts + jnp.sum(chosen_f, axis=0, keepdims=True)

    slot = lax.broadcasted_iota(jnp.int32, (tt, SLOTS), 1)
    s_cols = [jnp.sum(jnp.where(oh, scores, 0.0), axis=-1, keepdims=True) for oh in onehots]
    denom = s_cols[0]
    for s in s_cols[1:]:
        denom = denom + s
    e_tab = jnp.zeros((tt, SLOTS), jnp.int32)
    r_tab = jnp.zeros((tt, SLOTS), jnp.int32)
    w_tab = jnp.zeros((tt, SLOTS), F32)
    for kk, oh in enumerate(onehots):
        e_col = jnp.sum(jnp.where(oh, lane, 0), axis=-1, keepdims=True)
        r_col = jnp.sum(jnp.where(oh, rank, 0.0), axis=-1, keepdims=True).astype(jnp.int32)
        w_col = s_cols[kk] / denom * ROUTED_SCALE
        e_tab = jnp.where(slot == kk, e_col, e_tab)
        r_tab = jnp.where(slot == kk, r_col, r_tab)
        w_tab = jnp.where(slot == kk, w_col, w_tab)
    e_ref[...] = e_tab
    r_ref[...] = r_tab
    w_ref[...] = w_tab


def _router(h1, router_w, router_b):
    tokens, d_model = h1.shape
    n_exp = router_w.shape[1]
    tt = ROUTER_TILE
    assert tokens % tt == 0
    tab = lambda dt: jax.ShapeDtypeStruct((tokens, SLOTS), dt)
    tab_spec = pl.BlockSpec((tt, SLOTS), lambda t: (t, 0))
    return pl.pallas_call(
        _router_kernel,
        grid=(tokens // tt,),
        in_specs=[pl.BlockSpec((tt, d_model), lambda t: (t, 0)),
                  pl.BlockSpec((d_model, n_exp), lambda t: (0, 0)),
                  pl.BlockSpec((1, n_exp), lambda t: (0, 0))],
        out_specs=[tab_spec, tab_spec, tab_spec, pl.BlockSpec((1, n_exp), lambda t: (0, 0))],
        out_shape=[tab(jnp.int32), tab(jnp.int32), tab(F32), jax.ShapeDtypeStruct((1, n_exp), F32)],
        compiler_params=pltpu.CompilerParams(dimension_semantics=("arbitrary",), vmem_limit_bytes=VMEM_LIMIT),
        name="router",
    )(h1, router_w, router_b)


def _dispatch_kernel(tail_ref, na_ref, pos_ref, h_ref, xs_ref, zbuf, src, sem, zsem):
    step = pl.program_id(0)
    tt, d_model = h_ref.shape
    n_exp = tail_ref.shape[0]
    eb = zbuf.shape[0]
    n_blocks = xs_ref.shape[0] // eb
    dump_base = pl.num_programs(0) * tt * TOP_K

    @pl.when(step == 0)
    def _():
        zbuf[:, :d_model] = jnp.zeros((eb, d_model), F32)
        zbuf[:, d_model:] = (dump_base + lax.broadcasted_iota(jnp.int32, (eb, TAG_W), 0)).astype(F32)

        def tail_copy(e):
            start = pl.multiple_of(tail_ref[e], eb)
            return pltpu.make_async_copy(zbuf, xs_ref.at[pl.ds(start, eb), :], zsem)

        def tag_copy(j):
            start = pl.multiple_of(j * eb, eb)
            return pltpu.make_async_copy(zbuf.at[:, pl.ds(d_model, TAG_W)],
                                         xs_ref.at[pl.ds(start, eb), pl.ds(d_model, TAG_W)], zsem)

        def start_all(mk):
            def body(i, carry):
                mk(i).start()
                return carry
            return body

        def wait_all(mk):
            def body(i, carry):
                mk(i).wait()
                return carry
            return body

        lax.fori_loop(0, n_exp, start_all(tail_copy), 0)
        lax.fori_loop(na_ref[0], n_blocks, start_all(tag_copy), 0)
        lax.fori_loop(0, n_exp, wait_all(tail_copy), 0)
        lax.fori_loop(na_ref[0], n_blocks, wait_all(tag_copy), 0)

    h = h_ref[...]
    tok = step * tt + lax.broadcasted_iota(jnp.int32, (tt, TAG_W), 0)
    for kk in range(TOP_K):
        src[kk, :, :d_model] = h
        src[kk, :, d_model:] = (tok * TOP_K + kk).astype(F32)

    def issue(t, carry):
        for kk in range(TOP_K):
            p = pos_ref[t * SLOTS + kk]
            pltpu.make_async_copy(src.at[kk, pl.ds(t, 1), :], xs_ref.at[pl.ds(p, 1), :], sem).start(
                priority=kk % 2)
        return carry

    lax.fori_loop(0, tt, issue, 0)
    for kk in range(TOP_K):
        pltpu.make_async_copy(src.at[kk], xs_ref.at[pl.ds(0, tt), :], sem).wait()


def _dispatch(tails, n_active, pos_flat, h1, rows_total):
    tokens, d_model = h1.shape
    tt = DISPATCH_TILE
    assert tokens % tt == 0
    width = d_model + TAG_W
    grid_spec = pltpu.PrefetchScalarGridSpec(
        num_scalar_prefetch=2,
        grid=(tokens // tt,),
        in_specs=[pl.BlockSpec((tt * SLOTS,), lambda t, tl, na: (t,), memory_space=pltpu.SMEM),
                  pl.BlockSpec((tt, d_model), lambda t, tl, na: (t, 0))],
        out_specs=pl.BlockSpec(memory_space=pl.ANY),
        scratch_shapes=[pltpu.VMEM((EXPERT_ROWS, width), F32),
                        pltpu.VMEM((TOP_K, tt, width), F32),
                        pltpu.SemaphoreType.DMA, pltpu.SemaphoreType.DMA],
    )
    return pl.pallas_call(
        _dispatch_kernel,
        grid_spec=grid_spec,
        out_shape=jax.ShapeDtypeStruct((rows_total, width), F32),
        compiler_params=pltpu.CompilerParams(dimension_semantics=("arbitrary",), vmem_limit_bytes=VMEM_LIMIT),
        name="dispatch",
    )(tails, n_active, pos_flat, h1)


def _slots_kernel(tag_ref, slot_ref):
    for g in range(tag_ref.shape[0] // TAG_W):
        tile = tag_ref[g * TAG_W:(g + 1) * TAG_W, :]
        slot_ref[g:g + 1, :] = tile.T[0:1, :].astype(jnp.int32)


def _return_slots(xs):
    rows_total, width = xs.shape
    rb = next(r for r in (SLOT_ROWS, SLOT_ROWS // 2, SLOT_ROWS // 4, EXPERT_ROWS) if rows_total % r == 0)
    assert (width - TAG_W) % TAG_W == 0
    tag_block = (width - TAG_W) // TAG_W
    out = pl.pallas_call(
        _slots_kernel,
        grid=(rows_total // rb,),
        in_specs=[pl.BlockSpec((rb, TAG_W), lambda i: (i, tag_block))],
        out_specs=pl.BlockSpec((rb // TAG_W, TAG_W), lambda i: (i, 0)),
        out_shape=jax.ShapeDtypeStruct((rows_total // TAG_W, TAG_W), jnp.int32),
        compiler_params=pltpu.CompilerParams(dimension_semantics=("arbitrary",), vmem_limit_bytes=VMEM_LIMIT),
        name="return_slots",
    )(xs)
    return out.reshape(rows_total)


def _expert_kernel(be_ref, na_ref, slot_ref, xs_ref, wgu_ref, wd_ref, yk_ref, ybuf, sems):
    del be_ref
    j = pl.program_id(0)
    n_active = na_ref[0]
    eb = ybuf.shape[1]
    d_model = ybuf.shape[2]
    d_exp = wd_ref.shape[0]
    cur = j % 2

    def drain(buf):
        pltpu.make_async_copy(ybuf.at[buf], yk_ref.at[pl.ds(0, eb), :], sems.at[buf]).wait()

    @pl.when(j < n_active)
    def _():
        @pl.when(j >= 2)
        def _():
            drain(cur)

        gu = _dot(xs_ref[:, :d_model].astype(BF16), wgu_ref[...])
        act = _silu(gu[:, :d_exp]) * gu[:, d_exp:]
        ybuf[cur] = _dot(act.astype(BF16), wd_ref[...])

        def issue(r, carry):
            for u in range(ROW_UNROLL):
                row = r * ROW_UNROLL + u
                pltpu.make_async_copy(ybuf.at[cur, pl.ds(row, 1), :], yk_ref.at[pl.ds(slot_ref[row], 1), :],
                                      sems.at[cur]).start(priority=u % 2)
            return carry

        lax.fori_loop(0, eb // ROW_UNROLL, issue, 0)

    @pl.when(j == pl.num_programs(0) - 1)
    def _():
        drain((n_active - 1) % 2)

        @pl.when(n_active >= 2)
        def _():
            drain(n_active % 2)


def _experts(block_expert, n_active, slots, xs, w_gu, w_down, out_rows):
    rows_total, width = xs.shape
    d_model = width - TAG_W
    eb = EXPERT_ROWS
    n_blocks = rows_total // eb
    d_exp = w_down.shape[1]

    def blk(j, be, na):
        return jnp.minimum(j, na[0] - 1)

    def w_map(j, be, na):
        return (be[j], 0, 0)

    grid_spec = pltpu.PrefetchScalarGridSpec(
        num_scalar_prefetch=2,
        grid=(n_blocks,),
        in_specs=[pl.BlockSpec((eb,), lambda j, be, na: (blk(j, be, na),), memory_space=pltpu.SMEM),
                  pl.BlockSpec((eb, width), lambda j, be, na: (blk(j, be, na), 0)),
                  pl.BlockSpec((None, d_model, 2 * d_exp), w_map),
                  pl.BlockSpec((None, d_exp, d_model), w_map)],
        out_specs=pl.BlockSpec(memory_space=pl.ANY),
        scratch_shapes=[pltpu.VMEM((2, eb, d_model), F32), pltpu.SemaphoreType.DMA((2,))],
    )
    return pl.pallas_call(
        _expert_kernel,
        grid_spec=grid_spec,
        out_shape=jax.ShapeDtypeStruct((out_rows, d_model), F32),
        compiler_params=pltpu.CompilerParams(dimension_semantics=("arbitrary",), vmem_limit_bytes=VMEM_LIMIT),
        name="experts",
    )(block_expert, n_active, slots, xs, w_gu, w_down)


def _combine_kernel(h_ref, w_ref, yk_ref, wsgu_ref, wsd_ref, l2g_ref, l2b_ref, out_ref):
    d_model = h_ref.shape[1]
    d_sh = wsd_ref.shape[0]
    h = h_ref[...]
    gu = _dot(h.astype(BF16), wsgu_ref[...])
    act = _silu(gu[:, :d_sh]) * gu[:, d_sh:]
    acc = DEEPNORM_ALPHA * h + _dot(act.astype(BF16), wsd_ref[...])
    w = w_ref[...]
    for kk in range(TOP_K):
        acc = acc + w[:, kk:kk + 1] * yk_ref[:, kk * d_model:(kk + 1) * d_model]
    out_ref[...] = _layer_norm(acc, l2g_ref[...], l2b_ref[...])


def _combine(h1, w_tab, yk, ws_gu, ws_down, l2g, l2b):
    tokens, d_model = h1.shape
    tt = COMBINE_TILE
    assert tokens % tt == 0
    full = lambda a: pl.BlockSpec(a.shape, lambda t: (0,) * a.ndim)
    return pl.pallas_call(
        _combine_kernel,
        grid=(tokens // tt,),
        in_specs=[pl.BlockSpec((tt, d_model), lambda t: (t, 0)),
                  pl.BlockSpec((tt, SLOTS), lambda t: (t, 0)),
                  pl.BlockSpec((tt, TOP_K * d_model), lambda t: (t, 0)),
                  full(ws_gu), full(ws_down), full(l2g), full(l2b)],
        out_specs=pl.BlockSpec((tt, d_model), lambda t: (t, 0)),
        out_shape=jax.ShapeDtypeStruct((tokens, d_model), F32),
        compiler_params=pltpu.CompilerParams(dimension_semantics=("arbitrary",), vmem_limit_bytes=VMEM_LIMIT),
        name="combine",
    )(h1, w_tab, yk, ws_gu, ws_down, l2g, l2b)


def kernel(x, meta_tokens, ln_in_g, ln_in_b, w_in, b_in, hg_lb_logits, hg_norm_g, conv_w, conv_b, lru_wa, lru_ba,
           lru_wx, lru_bx, lru_lambda, w_out, ln1_g, ln1_b, router_w, router_bias, we_gate, we_up, we_down,
           ws_gate, ws_up, ws_down, ln2_g, ln2_b):
    batch, seq, d_model = x.shape
    assert w_in.shape[0] == DEPTH
    n_exp = router_w.shape[-1]
    row = lambda a: a.reshape(1, -1).astype(F32)

    meta_pad = jnp.concatenate([jnp.zeros((ROW_BLOCK - N_META, d_model), F32), meta_tokens.astype(F32)], axis=0)
    w_lru = jnp.concatenate([lru_wa[0], lru_wx[0]], axis=-1).astype(BF16)
    h1 = _mixer(x, meta_pad, row(ln_in_g), row(ln_in_b), w_in[0].astype(BF16), row(b_in[0]),
                hg_lb_logits.astype(F32), row(hg_norm_g[0]), conv_w[0].astype(F32), row(conv_b[0]), w_lru,
                row(lru_ba[0]), row(lru_bx[0]), row(lru_lambda[0]), w_out[0].astype(BF16),
                row(ln1_g[0]), row(ln1_b[0]))
    tokens = batch * seq
    h1 = h1.reshape(tokens, d_model)

    e_tab, r_tab, w_tab, counts = _router(h1, router_w[0].astype(F32), row(router_bias[0]))

    eb = EXPERT_ROWS
    n_blocks = tokens * TOP_K // eb + n_exp
    counts = counts.reshape(n_exp).astype(jnp.int32)
    padded = (counts + eb - 1) // eb * eb
    pad_ends = jnp.cumsum(padded)
    pad_starts = pad_ends - padded
    n_active = (pad_ends[-1] // eb).astype(jnp.int32)
    blk = jnp.minimum(jnp.arange(n_blocks, dtype=jnp.int32), n_active - 1)
    block_expert = jnp.sum((pad_ends[None, :] <= (blk * eb)[:, None]).astype(jnp.int32), axis=1)
    block_expert = jnp.minimum(block_expert, n_exp - 1)
    pos = pad_starts[e_tab] + r_tab
    pos_flat = pos.reshape(tokens * SLOTS).astype(jnp.int32)
    tails = jnp.maximum(pad_ends - eb, 0).astype(jnp.int32)

    n_active = n_active.reshape(1)
    xs = _dispatch(tails, n_active, pos_flat, h1, n_blocks * eb)
    slots = _return_slots(xs)
    w_gu = jnp.concatenate([we_gate[0], we_up[0]], axis=-1).astype(BF16)
    discard_tokens = -(-eb // TOP_K)
    yk = _experts(block_expert, n_active, slots, xs, w_gu, we_down[0].astype(BF16),
                  (tokens + discard_tokens) * TOP_K)
    yk = yk.reshape(tokens + discard_tokens, TOP_K * d_model)
    ws_gu = jnp.concatenate([ws_gate[0], ws_up[0]], axis=-1).astype(BF16)
    out = _combine(h1, w_tab, yk, ws_gu, ws_down[0].astype(BF16), row(ln2_g[0]), row(ln2_b[0]))
    return out.reshape(batch, seq, d_model)
```

```python
import jax
import jax.numpy as jnp
from jax import lax
from jax.experimental import pallas as pl
from jax.experimental.pallas import tpu as pltpu

F32 = jnp.float32
BF16 = jnp.bfloat16

N_META = 16
LN_EPS = 1e-5
RMS_EPS = 1e-6
HG_HEADS = 8
HEAD_DIM = 128
LRU_BLOCKS = 8
LRU_BLOCK = 128
CONV_W = 4
LRU_C = 8.0
TOP_K = 6
ROUTED_SCALE = 2.5
DEPTH = 1
DEEPNORM_ALPHA = (2.0 * DEPTH) ** 0.25
LOG2_E = 1.4426950408889634

ROW_BLOCK = 256
CHUNK = 128
ROUTER_TILE = 512
DISPATCH_TILE = 512
COMBINE_TILE = 256
EXPERT_ROWS = 512
SLOTS = 8
SUBLANES = 8
LANES = 128
VMEM_LIMIT = 56 * 1024 * 1024


def _layer_norm(x, g, b):
    mu = jnp.mean(x, axis=-1, keepdims=True)
    xc = x - mu
    var = jnp.mean(xc * xc, axis=-1, keepdims=True)
    return xc * lax.rsqrt(var + LN_EPS) * g + b


def _sigmoid(x):
    return 1.0 / (1.0 + jnp.exp2(x * (-LOG2_E)))


def _silu(x):
    return x * _sigmoid(x)


def _gelu_tanh(x):
    return 0.5 * x * (1.0 + jnp.tanh(0.7978845608028654 * (x + 0.044715 * (x * x * x))))


def _dot(a, b):
    return jnp.dot(a, b, preferred_element_type=F32)


def _dot_nt(a, b):
    return lax.dot_general(a, b, (((1,), (1,)), ((), ())), preferred_element_type=F32)


def _split3(x):
    hi = x.astype(BF16)
    r1 = x - hi.astype(F32)
    mid = r1.astype(BF16)
    lo = (r1 - mid.astype(F32)).astype(BF16)
    return hi, mid, lo


def _linear_scan(a, u, h0):
    rows, width = a.shape
    groups = rows // SUBLANES
    a3 = a.reshape(groups, SUBLANES, width)
    u3 = u.reshape(groups, SUBLANES, width)
    sid = lax.broadcasted_iota(jnp.int32, (groups, SUBLANES, width), 1)
    d = 1
    while d < SUBLANES:
        keep = sid >= d
        a_sh = jnp.where(keep, pltpu.roll(a3, d, 1), 1.0)
        u_sh = jnp.where(keep, pltpu.roll(u3, d, 1), 0.0)
        u3 = u3 + a3 * u_sh
        a3 = a3 * a_sh
        d *= 2
    out = []
    carry = h0
    for g in range(groups):
        hg = u3[g] + a3[g] * carry
        out.append(hg)
        carry = hg[SUBLANES - 1:SUBLANES, :]
    return jnp.concatenate(out, axis=0)


def _mixer_kernel(x_ref, meta_ref, lng_ref, lnb_ref, win_ref, bin_ref, lbl_ref, hgn_ref, cw_ref, cb_ref,
                  wlru_ref, ba_ref, bx_ref, lam_ref, wout_ref, l1g_ref, l1b_ref,
                  h1_ref,
                  st_ref, cwin_ref, hcar_ref, yhg_ref, st0_ref, cwin0_ref, hcar0_ref):
    bi = pl.program_id(0)
    i = pl.program_id(1)
    is_first = i == 0

    @pl.when(jnp.logical_and(is_first, bi == 0))
    def _():
        st_ref[...] = jnp.zeros_like(st_ref)
        cwin_ref[0:SUBLANES, :] = jnp.zeros_like(cwin0_ref)
        hcar_ref[...] = jnp.zeros_like(hcar_ref)

    @pl.when(jnp.logical_and(is_first, bi > 0))
    def _():
        st_ref[...] = st0_ref[...]
        cwin_ref[0:SUBLANES, :] = cwin0_ref[...]
        hcar_ref[...] = hcar0_ref[...]

    @pl.when(jnp.logical_or(jnp.logical_not(is_first), bi == 0))
    def _():
        _mixer_block(x_ref, meta_ref, lng_ref, lnb_ref, win_ref, bin_ref, lbl_ref, hgn_ref, cw_ref, cb_ref,
                     wlru_ref, ba_ref, bx_ref, lam_ref, wout_ref, l1g_ref, l1b_ref, h1_ref,
                     st_ref, cwin_ref, hcar_ref, yhg_ref, is_first)

        @pl.when(is_first)
        def _():
            st0_ref[...] = st_ref[...]
            cwin0_ref[...] = cwin_ref[0:SUBLANES, :]
            hcar0_ref[...] = hcar_ref[...]


def _mixer_block(x_ref, meta_ref, lng_ref, lnb_ref, win_ref, bin_ref, lbl_ref, hgn_ref, cw_ref, cb_ref,
                 wlru_ref, ba_ref, bx_ref, lam_ref, wout_ref, l1g_ref, l1b_ref, h1_ref,
                 st_ref, cwin_ref, hcar_ref, yhg_ref, is_first):
    rows = x_ref.shape[0]
    d_model = x_ref.shape[1]
    rid = lax.broadcasted_iota(jnp.int32, (rows, 1), 0)
    valid = jnp.logical_or(rid >= rows - N_META, jnp.logical_not(is_first))
    xin = jnp.where(is_first, meta_ref[...], x_ref[...])
    h = _layer_norm(xin, lng_ref[...], lnb_ref[...])
    hb = h.astype(BF16)

    def proj(seg):
        cols = slice(seg * d_model, (seg + 1) * d_model)
        return _dot(hb, win_ref[:, cols]) + bin_ref[:, cols]

    lbl = lbl_ref[...]
    lmax = jnp.max(lbl, axis=0, keepdims=True)
    lexp = jnp.exp(lbl - lmax)
    lb = lexp[0:1, :] / jnp.sum(lexp, axis=0, keepdims=True)

    q = _silu(proj(0))
    f = lb + (1.0 - lb) * _sigmoid(proj(1))
    lf = jnp.where(valid, jnp.log(f), 0.0)
    k = jnp.where(valid, 1.0 - f, 0.0)
    v = proj(2)

    r_i = lax.broadcasted_iota(jnp.int32, (CHUNK, CHUNK), 0)
    c_i = lax.broadcasted_iota(jnp.int32, (CHUNK, CHUNK), 1)
    causal = r_i >= c_i
    tri = jnp.where(causal, 1.0, 0.0).astype(BF16)
    hgn = hgn_ref[...]
    half = CHUNK // 2
    for c in range(rows // CHUNK):
        rs = slice(c * CHUNK, (c + 1) * CHUNK)
        lf_hi, lf_mid, lf_lo = _split3(lf[rs])
        b = _dot(tri, lf_hi) + _dot(tri, lf_mid) + _dot(tri, lf_lo)
        b_mid = b[half - 1:half, :]
        b_last = b[CHUNK - 1:CHUNK, :]
        q_t = q[rs] * jnp.exp(b - b_mid)
        k_t = k[rs] * jnp.exp(b_mid - b)
        q_s = (q_t * jnp.exp(b_mid)).astype(BF16)
        k_s = (k_t * jnp.exp(b_last - b_mid)).astype(BF16)
        q_t = q_t.astype(BF16)
        k_t = k_t.astype(BF16)
        s_decay = jnp.exp(b_last)
        vc = v[rs]
        vb = vc.astype(BF16)
        for hd in range(HG_HEADS):
            cs = slice(hd * HEAD_DIM, (hd + 1) * HEAD_DIM)
            p = jnp.where(causal, _dot_nt(q_t[:, cs], k_t[:, cs]), 0.0)
            st = st_ref[hd]
            o = _dot(p.astype(BF16), vb[:, cs]) + _dot_nt(q_s[:, cs], st.astype(BF16))
            st_ref[hd] = st * s_decay[:, cs] + _dot(vc[:, cs].T.astype(BF16), k_s[:, cs])
            o = o * lax.rsqrt(jnp.mean(o * o, axis=-1, keepdims=True) + RMS_EPS) * hgn
            yhg_ref[rs, cs] = o
    y_hg = yhg_ref[...] * _silu(proj(3))

    lx = jnp.where(valid, proj(4), 0.0)
    cwin_ref[SUBLANES:SUBLANES + rows, :] = lx
    cw = cw_ref[...]
    first_tap = SUBLANES - (CONV_W - 1)
    xc = cb_ref[...] + cw[0:1, :] * cwin_ref[first_tap:first_tap + rows, :]
    for j in range(1, CONV_W):
        xc = xc + cw[j:j + 1, :] * cwin_ref[first_tap + j:first_tap + j + rows, :]
    cwin_ref[0:SUBLANES, :] = lx[rows - SUBLANES:rows, :]

    ra_parts, ri_parts = [], []
    for n in range(LRU_BLOCKS):
        cs = slice(n * LRU_BLOCK, (n + 1) * LRU_BLOCK)
        g2 = _dot(xc[:, cs].astype(BF16), wlru_ref[n])
        ra_parts.append(g2[:, :LRU_BLOCK])
        ri_parts.append(g2[:, LRU_BLOCK:])
    r_gate = _sigmoid(jnp.concatenate(ra_parts, axis=1) + ba_ref[...])
    i_gate = _sigmoid(jnp.concatenate(ri_parts, axis=1) + bx_ref[...])
    neg = -lam_ref[...]
    softplus_neg = jnp.maximum(neg, 0.0) + jnp.log(1.0 + jnp.exp(-jnp.abs(neg)))
    rate = (-LRU_C * LOG2_E) * softplus_neg
    a = jnp.exp2(r_gate * rate)
    u = jnp.where(valid, jnp.sqrt(1.0 - a * a) * (i_gate * xc), 0.0)
    h_lru = _linear_scan(a, u, hcar_ref[0:1, :])
    hcar_ref[0:1, :] = h_lru[rows - 1:rows, :]
    y_lru = h_lru * _gelu_tanh(proj(5))

    merged = _sigmoid(proj(6)) * y_hg + _sigmoid(proj(7)) * y_lru
    mix = _dot(merged.astype(BF16), wout_ref[...])
    h1_ref[...] = _layer_norm(DEEPNORM_ALPHA * h + mix, l1g_ref[...], l1b_ref[...])


def _const_spec(shape):
    nd = len(shape)
    return pl.BlockSpec(shape, lambda b, i: (0,) * nd, pipeline_mode=pl.Buffered(1))


def _mixer(x, meta_pad, ln_g, ln_b, w_in, b_in, lb_logits, hg_norm, conv_w, conv_b, w_lru, ba, bx, lam,
           w_out, l1g, l1b):
    batch, seq, d_model = x.shape
    rows = ROW_BLOCK
    assert seq % rows == 0 and d_model == HG_HEADS * HEAD_DIM == LRU_BLOCKS * LRU_BLOCK
    n_blk = seq // rows + 1
    consts = (meta_pad, ln_g, ln_b, w_in, b_in, lb_logits, hg_norm, conv_w, conv_b, w_lru, ba, bx, lam,
              w_out, l1g, l1b)
    x_spec = pl.BlockSpec((None, rows, d_model), lambda b, i: (b, jnp.maximum(i - 1, 0), 0))
    return pl.pallas_call(
        _mixer_kernel,
        grid=(batch, n_blk),
        in_specs=[x_spec] + [_const_spec(c.shape) for c in consts],
        out_specs=pl.BlockSpec((None, rows, d_model), lambda b, i: (b, jnp.maximum(i - 1, 0), 0)),
        out_shape=jax.ShapeDtypeStruct((batch, seq, d_model), F32),
        scratch_shapes=[
            pltpu.VMEM((HG_HEADS, HEAD_DIM, HEAD_DIM), F32),
            pltpu.VMEM((rows + SUBLANES, d_model), F32),
            pltpu.VMEM((SUBLANES, d_model), F32),
            pltpu.VMEM((rows, d_model), F32),
            pltpu.VMEM((HG_HEADS, HEAD_DIM, HEAD_DIM), F32),
            pltpu.VMEM((SUBLANES, d_model), F32),
            pltpu.VMEM((SUBLANES, d_model), F32),
        ],
        compiler_params=pltpu.CompilerParams(
            dimension_semantics=("arbitrary", "arbitrary"), vmem_limit_bytes=VMEM_LIMIT),
        name="mixer",
    )(x, *consts)


def _router_kernel(h_ref, rw_ref, rb_ref, e_ref, r_ref, w_ref, cnt_ref):
    step = pl.program_id(0)

    @pl.when(step == 0)
    def _():
        cnt_ref[...] = jnp.zeros_like(cnt_ref)

    h = h_ref[...]
    tt = h.shape[0]
    n_exp = cnt_ref.shape[0]
    h_hi = h.astype(BF16)
    h_lo = (h - h_hi.astype(F32)).astype(BF16)
    rw = rw_ref[...]
    w_hi = rw.astype(BF16)
    w_lo = (rw - w_hi.astype(F32)).astype(BF16)
    logits = _dot(h_hi, w_hi) + _dot(h_hi, w_lo) + _dot(h_lo, w_hi)
    scores = _sigmoid(logits.T[0:n_exp, :])
    sel = scores + rb_ref[...]
    eid = lax.broadcasted_iota(jnp.int32, (n_exp, tt), 0)
    onehots, picks = [], []
    for _ in range(TOP_K):
        m = jnp.max(sel, axis=0, keepdims=True)
        idx = jnp.min(jnp.where(sel == m, eid, n_exp), axis=0, keepdims=True)
        oh = eid == idx
        onehots.append(oh)
        picks.append(idx)
        sel = jnp.where(oh, -jnp.inf, sel)
    chosen = onehots[0]
    for oh in onehots[1:]:
        chosen = jnp.logical_or(chosen, oh)
    chosen_f = jnp.where(chosen, 1.0, 0.0)
    r_i = lax.broadcasted_iota(jnp.int32, (tt, tt), 0)
    c_i = lax.broadcasted_iota(jnp.int32, (tt, tt), 1)
    earlier = jnp.where(r_i < c_i, 1.0, 0.0).astype(BF16)
    counts = cnt_ref[...]
    rank = _dot(chosen_f.astype(BF16), earlier) + counts[:, 0:1]
    cnt_ref[...] = counts + jnp.sum(chosen_f, axis=1, keepdims=True)

    s_rows = [jnp.sum(jnp.where(oh, scores, 0.0), axis=0, keepdims=True) for oh in onehots]
    denom = s_rows[0]
    for s in s_rows[1:]:
        denom = denom + s
    for kk, oh in enumerate(onehots):
        e_ref[kk:kk + 1, :] = picks[kk]
        r_ref[kk:kk + 1, :] = jnp.sum(jnp.where(oh, rank, 0.0), axis=0, keepdims=True).astype(jnp.int32)
        w_ref[kk:kk + 1, :] = s_rows[kk] / denom * ROUTED_SCALE
    e_ref[TOP_K:SLOTS, :] = jnp.zeros((SLOTS - TOP_K, tt), jnp.int32)
    r_ref[TOP_K:SLOTS, :] = jnp.zeros((SLOTS - TOP_K, tt), jnp.int32)
    w_ref[TOP_K:SLOTS, :] = jnp.zeros((SLOTS - TOP_K, tt), F32)


def _router(h1, router_w, router_b):
    tokens, d_model = h1.shape
    n_exp = router_w.shape[1]
    tt = ROUTER_TILE
    assert tokens % tt == 0 and n_exp <= LANES and n_exp % SUBLANES == 0
    rw_pad = jnp.pad(router_w, ((0, 0), (0, LANES - n_exp)))
    rb_cols = jnp.broadcast_to(router_b.reshape(n_exp, 1), (n_exp, tt))
    tab = lambda dt: jax.ShapeDtypeStruct((SLOTS, tokens), dt)
    tab_spec = pl.BlockSpec((SLOTS, tt), lambda t: (0, t))
    return pl.pallas_call(
        _router_kernel,
        grid=(tokens // tt,),
        in_specs=[pl.BlockSpec((tt, d_model), lambda t: (t, 0)),
                  pl.BlockSpec((d_model, LANES), lambda t: (0, 0)),
                  pl.BlockSpec((n_exp, tt), lambda t: (0, 0))],
        out_specs=[tab_spec, tab_spec, tab_spec, pl.BlockSpec((n_exp, LANES), lambda t: (0, 0))],
        out_shape=[tab(jnp.int32), tab(jnp.int32), tab(F32), jax.ShapeDtypeStruct((n_exp, LANES), F32)],
        compiler_params=pltpu.CompilerParams(dimension_semantics=("arbitrary",), vmem_limit_bytes=VMEM_LIMIT),
        name="router",
    )(h1, rw_pad, rb_cols)


def _dispatch_kernel(tail_ref, pos_ref, h_ref, xs_ref, zbuf, sem, zsem):
    tt = h_ref.shape[0]
    n_exp = tail_ref.shape[0]

    @pl.when(pl.program_id(0) == 0)
    def _():
        zbuf[...] = jnp.zeros_like(zbuf)

        def tail_copy(e):
            start = pl.multiple_of(tail_ref[e], zbuf.shape[0])
            return pltpu.make_async_copy(zbuf, xs_ref.at[pl.ds(start, zbuf.shape[0]), :], zsem)

        def zstart(e, carry):
            tail_copy(e).start()
            return carry

        def zwait(e, carry):
            tail_copy(e).wait()
            return carry

        lax.fori_loop(0, n_exp, zstart, 0)
        lax.fori_loop(0, n_exp, zwait, 0)

    def issue(t, carry):
        for kk in range(TOP_K):
            p = pos_ref[t * SLOTS + kk]
            pltpu.make_async_copy(h_ref.at[pl.ds(t, 1), :], xs_ref.at[pl.ds(p, 1), :], sem).start(priority=kk % 2)
        return carry

    lax.fori_loop(0, tt, issue, 0)
    for kk in range(TOP_K):
        pltpu.make_async_copy(h_ref, xs_ref.at[pl.ds(0, tt), :], sem).wait()


def _dispatch(tails, pos_flat, h1, rows_total):
    tokens, d_model = h1.shape
    tt = DISPATCH_TILE
    assert tokens % tt == 0
    grid_spec = pltpu.PrefetchScalarGridSpec(
        num_scalar_prefetch=1,
        grid=(tokens // tt,),
        in_specs=[pl.BlockSpec((tt * SLOTS,), lambda t, tl: (t,), memory_space=pltpu.SMEM),
                  pl.BlockSpec((tt, d_model), lambda t, tl: (t, 0))],
        out_specs=pl.BlockSpec(memory_space=pl.ANY),
        scratch_shapes=[pltpu.VMEM((EXPERT_ROWS, d_model), F32),
                        pltpu.SemaphoreType.DMA, pltpu.SemaphoreType.DMA],
    )
    return pl.pallas_call(
        _dispatch_kernel,
        grid_spec=grid_spec,
        out_shape=jax.ShapeDtypeStruct((rows_total, d_model), F32),
        compiler_params=pltpu.CompilerParams(dimension_semantics=("arbitrary",), vmem_limit_bytes=VMEM_LIMIT),
        name="dispatch",
    )(tails, pos_flat, h1)


def _expert_kernel(be_ref, na_ref, xs_ref, wg_ref, wu_ref, wd_ref, ys_ref, wgu_bf, wd_bf):
    j = pl.program_id(0)
    d_exp = wd_ref.shape[0]

    @pl.when(jnp.logical_or(j == 0, be_ref[j] != be_ref[jnp.maximum(j - 1, 0)]))
    def _():
        wgu_bf[:, :d_exp] = wg_ref[...].astype(BF16)
        wgu_bf[:, d_exp:] = wu_ref[...].astype(BF16)
        wd_bf[...] = wd_ref[...].astype(BF16)

    @pl.when(j < na_ref[0])
    def _():
        gu = _dot(xs_ref[...].astype(BF16), wgu_bf[...])
        act = _silu(gu[:, :d_exp]) * gu[:, d_exp:]
        ys_ref[...] = _dot(act.astype(BF16), wd_bf[...])


def _experts(block_expert, n_active, xs, w_gate, w_up, w_down):
    rows_total, d_model = xs.shape
    eb = EXPERT_ROWS
    n_blocks = rows_total // eb
    d_exp = w_down.shape[1]

    def row_map(j, be, na):
        return (jnp.minimum(j, na[0] - 1), 0)

    def w_map(j, be, na):
        return (be[j], 0, 0)

    grid_spec = pltpu.PrefetchScalarGridSpec(
        num_scalar_prefetch=2,
        grid=(n_blocks,),
        in_specs=[pl.BlockSpec((eb, d_model), row_map),
                  pl.BlockSpec((None, d_model, d_exp), w_map),
                  pl.BlockSpec((None, d_model, d_exp), w_map),
                  pl.BlockSpec((None, d_exp, d_model), w_map)],
        out_specs=pl.BlockSpec((eb, d_model), row_map),
        scratch_shapes=[pltpu.VMEM((d_model, 2 * d_exp), BF16), pltpu.VMEM((d_exp, d_model), BF16)],
    )
    return pl.pallas_call(
        _expert_kernel,
        grid_spec=grid_spec,
        out_shape=jax.ShapeDtypeStruct((rows_total, d_model), F32),
        compiler_params=pltpu.CompilerParams(dimension_semantics=("arbitrary",), vmem_limit_bytes=VMEM_LIMIT),
        name="experts",
    )(block_expert, n_active, xs, w_gate, w_up, w_down)


def _combine_kernel(pos_ref, nxt_ref, h_ref, w_ref, wsgu_ref, wsd_ref, l2g_ref, l2b_ref, ys_ref, out_ref,
                    gbuf, sems):
    step = pl.program_id(0)
    n_steps = pl.num_programs(0)
    tt = h_ref.shape[0]
    d_sh = wsd_ref.shape[0]
    cur = step % 2

    def gather(table_ref, buf):
        def issue(t, carry):
            for kk in range(TOP_K):
                p = table_ref[t * SLOTS + kk]
                pltpu.make_async_copy(ys_ref.at[pl.ds(p, 1), :], gbuf.at[buf, kk, pl.ds(t, 1), :],
                                      sems.at[buf]).start(priority=kk % 2)
            return carry

        lax.fori_loop(0, tt, issue, 0)

    @pl.when(step == 0)
    def _():
        gather(pos_ref, 0)

    @pl.when(step + 1 < n_steps)
    def _():
        gather(nxt_ref, 1 - cur)

    h = h_ref[...]
    gu = _dot(h.astype(BF16), wsgu_ref[...])
    act = _silu(gu[:, :d_sh]) * gu[:, d_sh:]
    acc = DEEPNORM_ALPHA * h + _dot(act.astype(BF16), wsd_ref[...])

    for kk in range(TOP_K):
        pltpu.make_async_copy(ys_ref.at[pl.ds(0, tt), :], gbuf.at[cur, kk], sems.at[cur]).wait()

    w = w_ref[...]
    for kk in range(TOP_K):
        acc = acc + w[:, kk:kk + 1] * gbuf[cur, kk]
    out_ref[...] = _layer_norm(acc, l2g_ref[...], l2b_ref[...])


def _combine(pos_flat, h1, w_tok, ws_gu, ws_down, l2g, l2b, ys):
    tokens, d_model = h1.shape
    tt = COMBINE_TILE
    assert tokens % tt == 0
    full = lambda a: pl.BlockSpec(a.shape, lambda t: (0,) * a.ndim)
    n_tiles = tokens // tt
    return pl.pallas_call(
        _combine_kernel,
        grid=(n_tiles,),
        in_specs=[pl.BlockSpec((tt * SLOTS,), lambda t: (t,), memory_space=pltpu.SMEM),
                  pl.BlockSpec((tt * SLOTS,), lambda t: (jnp.minimum(t + 1, n_tiles - 1),),
                               memory_space=pltpu.SMEM),
                  pl.BlockSpec((tt, d_model), lambda t: (t, 0)),
                  pl.BlockSpec((tt, SLOTS), lambda t: (t, 0)),
                  full(ws_gu), full(ws_down), full(l2g), full(l2b),
                  pl.BlockSpec(memory_space=pl.ANY)],
        out_specs=pl.BlockSpec((tt, d_model), lambda t: (t, 0)),
        out_shape=jax.ShapeDtypeStruct((tokens, d_model), F32),
        scratch_shapes=[pltpu.VMEM((2, TOP_K, tt, d_model), F32), pltpu.SemaphoreType.DMA((2,))],
        compiler_params=pltpu.CompilerParams(dimension_semantics=("arbitrary",), vmem_limit_bytes=VMEM_LIMIT),
        name="combine",
    )(pos_flat, pos_flat, h1, w_tok, ws_gu, ws_down, l2g, l2b, ys)


def kernel(x, meta_tokens, ln_in_g, ln_in_b, w_in, b_in, hg_lb_logits, hg_norm_g, conv_w, conv_b, lru_wa, lru_ba,
           lru_wx, lru_bx, lru_lambda, w_out, ln1_g, ln1_b, router_w, router_bias, we_gate, we_up, we_down,
           ws_gate, ws_up, ws_down, ln2_g, ln2_b):
    batch, seq, d_model = x.shape
    assert w_in.shape[0] == DEPTH
    n_exp = router_w.shape[-1]
    row = lambda a: a.reshape(1, -1).astype(F32)

    meta_pad = jnp.concatenate([jnp.zeros((ROW_BLOCK - N_META, d_model), F32), meta_tokens.astype(F32)], axis=0)
    w_lru = jnp.concatenate([lru_wa[0], lru_wx[0]], axis=-1).astype(BF16)
    h1 = _mixer(x, meta_pad, row(ln_in_g), row(ln_in_b), w_in[0].astype(BF16), row(b_in[0]),
                hg_lb_logits.astype(F32), row(hg_norm_g[0]), conv_w[0].astype(F32), row(conv_b[0]), w_lru,
                row(lru_ba[0]), row(lru_bx[0]), row(lru_lambda[0]), w_out[0].astype(BF16),
                row(ln1_g[0]), row(ln1_b[0]))
    tokens = batch * seq
    h1 = h1.reshape(tokens, d_model)

    e_tab, r_tab, w_tab, counts = _router(h1, router_w[0].astype(F32), router_bias[0].astype(F32))

    eb = EXPERT_ROWS
    n_blocks = tokens * TOP_K // eb + n_exp
    counts = counts[:, 0].astype(jnp.int32)
    padded = (counts + eb - 1) // eb * eb
    pad_ends = jnp.cumsum(padded)
    pad_starts = pad_ends - padded
    n_active = (pad_ends[-1] // eb).astype(jnp.int32)
    blk = jnp.minimum(jnp.arange(n_blocks, dtype=jnp.int32), n_active - 1)
    block_expert = jnp.sum((pad_ends[None, :] <= (blk * eb)[:, None]).astype(jnp.int32), axis=1)
    block_expert = jnp.minimum(block_expert, n_exp - 1)
    pos = pad_starts[e_tab] + r_tab
    pos_flat = pos.T.reshape(tokens * SLOTS).astype(jnp.int32)
    tails = jnp.maximum(pad_ends - eb, 0).astype(jnp.int32)

    xs = _dispatch(tails, pos_flat, h1, n_blocks * eb)
    ys = _experts(block_expert, n_active.reshape(1), xs, we_gate[0].astype(F32), we_up[0].astype(F32),
                  we_down[0].astype(F32))
    ws_gu = jnp.concatenate([ws_gate[0], ws_up[0]], axis=-1).astype(BF16)
    out = _combine(pos_flat, h1, w_tab.T, ws_gu, ws_down[0].astype(BF16), row(ln2_g[0]), row(ln2_b[0]), ys)
    return out.reshape(batch, seq, d_model)
```

```python
import jax
import jax.numpy as jnp
from jax import lax
from jax.experimental import pallas as pl
from jax.experimental.pallas import tpu as pltpu

F32 = jnp.float32
BF16 = jnp.bfloat16

N_META = 16
LN_EPS = 1e-5
RMS_EPS = 1e-6
HG_HEADS = 8
HEAD_DIM = 128
LRU_BLOCKS = 8
LRU_BLOCK = 128
CONV_W = 4
LRU_C = 8.0
TOP_K = 6
ROUTED_SCALE = 2.5
DEPTH = 1
DEEPNORM_ALPHA = (2.0 * DEPTH) ** 0.25
LOG2_E = 1.4426950408889634

ROW_BLOCK = 256
CHUNK = 128
ROUTER_TILE = 512
POSITION_TILE = 1024
DISPATCH_TILE = 512
COMBINE_TILE = 256
EXPERT_ROWS = 512
SLOTS = 8
SUBLANES = 8
LANES = 128
VMEM_LIMIT = 56 * 1024 * 1024


def _layer_norm(x, g, b):
    mu = jnp.mean(x, axis=-1, keepdims=True)
    xc = x - mu
    var = jnp.mean(xc * xc, axis=-1, keepdims=True)
    return xc * lax.rsqrt(var + LN_EPS) * g + b


def _sigmoid(x):
    return 1.0 / (1.0 + jnp.exp2(x * (-LOG2_E)))


def _silu(x):
    return x * _sigmoid(x)


def _gelu_tanh(x):
    return 0.5 * x * (1.0 + jnp.tanh(0.7978845608028654 * (x + 0.044715 * (x * x * x))))


def _dot(a, b):
    return jnp.dot(a, b, preferred_element_type=F32)


def _dot_nt(a, b):
    return lax.dot_general(a, b, (((1,), (1,)), ((), ())), preferred_element_type=F32)


def _split3(x):
    hi = x.astype(BF16)
    r1 = x - hi.astype(F32)
    mid = r1.astype(BF16)
    lo = (r1 - mid.astype(F32)).astype(BF16)
    return hi, mid, lo


def _linear_scan(a, u, h0):
    rows, width = a.shape
    groups = rows // SUBLANES
    a3 = a.reshape(groups, SUBLANES, width)
    u3 = u.reshape(groups, SUBLANES, width)
    sid = lax.broadcasted_iota(jnp.int32, (groups, SUBLANES, width), 1)
    d = 1
    while d < SUBLANES:
        keep = sid >= d
        a_sh = jnp.where(keep, pltpu.roll(a3, d, 1), 1.0)
        u_sh = jnp.where(keep, pltpu.roll(u3, d, 1), 0.0)
        u3 = u3 + a3 * u_sh
        a3 = a3 * a_sh
        d *= 2
    out = []
    carry = h0
    for g in range(groups):
        hg = u3[g] + a3[g] * carry
        out.append(hg)
        carry = hg[SUBLANES - 1:SUBLANES, :]
    return jnp.concatenate(out, axis=0)


def _mixer_kernel(x_ref, meta_ref, lng_ref, lnb_ref, win_ref, bin_ref, lbl_ref, hgn_ref, cw_ref, cb_ref,
                  wlru_ref, ba_ref, bx_ref, lam_ref, wout_ref, l1g_ref, l1b_ref,
                  h1_ref,
                  st_ref, cwin_ref, hcar_ref, yhg_ref, st0_ref, cwin0_ref, hcar0_ref):
    bi = pl.program_id(0)
    i = pl.program_id(1)
    is_first = i == 0

    @pl.when(jnp.logical_and(is_first, bi == 0))
    def _():
        st_ref[...] = jnp.zeros_like(st_ref)
        cwin_ref[0:SUBLANES, :] = jnp.zeros_like(cwin0_ref)
        hcar_ref[...] = jnp.zeros_like(hcar_ref)

    @pl.when(jnp.logical_and(is_first, bi > 0))
    def _():
        st_ref[...] = st0_ref[...]
        cwin_ref[0:SUBLANES, :] = cwin0_ref[...]
        hcar_ref[...] = hcar0_ref[...]

    @pl.when(jnp.logical_or(jnp.logical_not(is_first), bi == 0))
    def _():
        _mixer_block(x_ref, meta_ref, lng_ref, lnb_ref, win_ref, bin_ref, lbl_ref, hgn_ref, cw_ref, cb_ref,
                     wlru_ref, ba_ref, bx_ref, lam_ref, wout_ref, l1g_ref, l1b_ref, h1_ref,
                     st_ref, cwin_ref, hcar_ref, yhg_ref, is_first)

        @pl.when(is_first)
        def _():
            st0_ref[...] = st_ref[...]
            cwin0_ref[...] = cwin_ref[0:SUBLANES, :]
            hcar0_ref[...] = hcar_ref[...]


def _mixer_block(x_ref, meta_ref, lng_ref, lnb_ref, win_ref, bin_ref, lbl_ref, hgn_ref, cw_ref, cb_ref,
                 wlru_ref, ba_ref, bx_ref, lam_ref, wout_ref, l1g_ref, l1b_ref, h1_ref,
                 st_ref, cwin_ref, hcar_ref, yhg_ref, is_first):
    rows = x_ref.shape[0]
    d_model = x_ref.shape[1]
    rid = lax.broadcasted_iota(jnp.int32, (rows, 1), 0)
    valid = jnp.logical_or(rid >= rows - N_META, jnp.logical_not(is_first))
    xin = jnp.where(is_first, meta_ref[...], x_ref[...])
    h = _layer_norm(xin, lng_ref[...], lnb_ref[...])
    hb = h.astype(BF16)

    def proj(seg):
        cols = slice(seg * d_model, (seg + 1) * d_model)
        return _dot(hb, win_ref[:, cols]) + bin_ref[:, cols]

    lbl = lbl_ref[...]
    lmax = jnp.max(lbl, axis=0, keepdims=True)
    lexp = jnp.exp(lbl - lmax)
    lb = lexp[0:1, :] / jnp.sum(lexp, axis=0, keepdims=True)

    q = _silu(proj(0))
    f = lb + (1.0 - lb) * _sigmoid(proj(1))
    lf = jnp.where(valid, jnp.log(f), 0.0)
    k = jnp.where(valid, 1.0 - f, 0.0)
    v = proj(2)

    r_i = lax.broadcasted_iota(jnp.int32, (CHUNK, CHUNK), 0)
    c_i = lax.broadcasted_iota(jnp.int32, (CHUNK, CHUNK), 1)
    causal = r_i >= c_i
    tri = jnp.where(causal, 1.0, 0.0).astype(BF16)
    hgn = hgn_ref[...]
    half = CHUNK // 2
    for c in range(rows // CHUNK):
        rs = slice(c * CHUNK, (c + 1) * CHUNK)
        lf_hi, lf_mid, lf_lo = _split3(lf[rs])
        b = _dot(tri, lf_hi) + _dot(tri, lf_mid) + _dot(tri, lf_lo)
        b_mid = b[half - 1:half, :]
        b_last = b[CHUNK - 1:CHUNK, :]
        q_t = q[rs] * jnp.exp(b - b_mid)
        k_t = k[rs] * jnp.exp(b_mid - b)
        q_s = (q_t * jnp.exp(b_mid)).astype(BF16)
        k_s = (k_t * jnp.exp(b_last - b_mid)).astype(BF16)
        q_t = q_t.astype(BF16)
        k_t = k_t.astype(BF16)
        s_decay = jnp.exp(b_last)
        vc = v[rs]
        vb = vc.astype(BF16)
        for hd in range(HG_HEADS):
            cs = slice(hd * HEAD_DIM, (hd + 1) * HEAD_DIM)
            p = jnp.where(causal, _dot_nt(q_t[:, cs], k_t[:, cs]), 0.0)
            st = st_ref[hd]
            o = _dot(p.astype(BF16), vb[:, cs]) + _dot_nt(q_s[:, cs], st.astype(BF16))
            st_ref[hd] = st * s_decay[:, cs] + _dot(vc[:, cs].T.astype(BF16), k_s[:, cs])
            o = o * lax.rsqrt(jnp.mean(o * o, axis=-1, keepdims=True) + RMS_EPS) * hgn
            yhg_ref[rs, cs] = o
    y_hg = yhg_ref[...] * _silu(proj(3))

    lx = jnp.where(valid, proj(4), 0.0)
    cwin_ref[SUBLANES:SUBLANES + rows, :] = lx
    cw = cw_ref[...]
    first_tap = SUBLANES - (CONV_W - 1)
    xc = cb_ref[...] + cw[0:1, :] * cwin_ref[first_tap:first_tap + rows, :]
    for j in range(1, CONV_W):
        xc = xc + cw[j:j + 1, :] * cwin_ref[first_tap + j:first_tap + j + rows, :]
    cwin_ref[0:SUBLANES, :] = lx[rows - SUBLANES:rows, :]

    ra_parts, ri_parts = [], []
    for n in range(LRU_BLOCKS):
        cs = slice(n * LRU_BLOCK, (n + 1) * LRU_BLOCK)
        g2 = _dot(xc[:, cs].astype(BF16), wlru_ref[n])
        ra_parts.append(g2[:, :LRU_BLOCK])
        ri_parts.append(g2[:, LRU_BLOCK:])
    r_gate = _sigmoid(jnp.concatenate(ra_parts, axis=1) + ba_ref[...])
    i_gate = _sigmoid(jnp.concatenate(ri_parts, axis=1) + bx_ref[...])
    neg = -lam_ref[...]
    softplus_neg = jnp.maximum(neg, 0.0) + jnp.log(1.0 + jnp.exp(-jnp.abs(neg)))
    rate = (-LRU_C * LOG2_E) * softplus_neg
    a = jnp.exp2(r_gate * rate)
    u = jnp.where(valid, jnp.sqrt(1.0 - a * a) * (i_gate * xc), 0.0)
    h_lru = _linear_scan(a, u, hcar_ref[0:1, :])
    hcar_ref[0:1, :] = h_lru[rows - 1:rows, :]
    y_lru = h_lru * _gelu_tanh(proj(5))

    merged = _sigmoid(proj(6)) * y_hg + _sigmoid(proj(7)) * y_lru
    mix = _dot(merged.astype(BF16), wout_ref[...])
    h1_ref[...] = _layer_norm(DEEPNORM_ALPHA * h + mix, l1g_ref[...], l1b_ref[...])


def _const_spec(shape):
    nd = len(shape)
    return pl.BlockSpec(shape, lambda b, i: (0,) * nd, pipeline_mode=pl.Buffered(1))


def _mixer(x, meta_pad, ln_g, ln_b, w_in, b_in, lb_logits, hg_norm, conv_w, conv_b, w_lru, ba, bx, lam,
           w_out, l1g, l1b):
    batch, seq, d_model = x.shape
    rows = ROW_BLOCK
    assert seq % rows == 0 and d_model == HG_HEADS * HEAD_DIM == LRU_BLOCKS * LRU_BLOCK
    n_blk = seq // rows + 1
    consts = (meta_pad, ln_g, ln_b, w_in, b_in, lb_logits, hg_norm, conv_w, conv_b, w_lru, ba, bx, lam,
              w_out, l1g, l1b)
    x_spec = pl.BlockSpec((None, rows, d_model), lambda b, i: (b, jnp.maximum(i - 1, 0), 0))
    return pl.pallas_call(
        _mixer_kernel,
        grid=(batch, n_blk),
        in_specs=[x_spec] + [_const_spec(c.shape) for c in consts],
        out_specs=pl.BlockSpec((None, rows, d_model), lambda b, i: (b, jnp.maximum(i - 1, 0), 0)),
        out_shape=jax.ShapeDtypeStruct((batch, seq, d_model), F32),
        scratch_shapes=[
            pltpu.VMEM((HG_HEADS, HEAD_DIM, HEAD_DIM), F32),
            pltpu.VMEM((rows + SUBLANES, d_model), F32),
            pltpu.VMEM((SUBLANES, d_model), F32),
            pltpu.VMEM((rows, d_model), F32),
            pltpu.VMEM((HG_HEADS, HEAD_DIM, HEAD_DIM), F32),
            pltpu.VMEM((SUBLANES, d_model), F32),
            pltpu.VMEM((SUBLANES, d_model), F32),
        ],
        compiler_params=pltpu.CompilerParams(
            dimension_semantics=("arbitrary", "arbitrary"), vmem_limit_bytes=VMEM_LIMIT),
        name="mixer",
    )(x, *consts)


def _router_kernel(h_ref, rw_ref, rb_ref, e_ref, r_ref, w_ref, cnt_ref):
    step = pl.program_id(0)

    @pl.when(step == 0)
    def _():
        cnt_ref[...] = jnp.zeros_like(cnt_ref)

    h = h_ref[...]
    tt = h.shape[0]
    n_exp = cnt_ref.shape[0]
    h_hi = h.astype(BF16)
    h_lo = (h - h_hi.astype(F32)).astype(BF16)
    rw = rw_ref[...]
    w_hi = rw.astype(BF16)
    w_lo = (rw - w_hi.astype(F32)).astype(BF16)
    logits = _dot(h_hi, w_hi) + _dot(h_hi, w_lo) + _dot(h_lo, w_hi)
    scores = _sigmoid(logits.T[0:n_exp, :])
    sel = scores + rb_ref[...]
    eid = lax.broadcasted_iota(jnp.int32, (n_exp, tt), 0)
    onehots, picks = [], []
    for _ in range(TOP_K):
        m = jnp.max(sel, axis=0, keepdims=True)
        idx = jnp.min(jnp.where(sel == m, eid, n_exp), axis=0, keepdims=True)
        oh = eid == idx
        onehots.append(oh)
        picks.append(idx)
        sel = jnp.where(oh, -jnp.inf, sel)
    chosen = onehots[0]
    for oh in onehots[1:]:
        chosen = jnp.logical_or(chosen, oh)
    chosen_f = jnp.where(chosen, 1.0, 0.0)
    r_i = lax.broadcasted_iota(jnp.int32, (tt, tt), 0)
    c_i = lax.broadcasted_iota(jnp.int32, (tt, tt), 1)
    earlier = jnp.where(r_i < c_i, 1.0, 0.0).astype(BF16)
    counts = cnt_ref[...]
    rank = _dot(chosen_f.astype(BF16), earlier) + counts[:, 0:1]
    cnt_ref[...] = counts + jnp.sum(chosen_f, axis=1, keepdims=True)

    s_rows = [jnp.sum(jnp.where(oh, scores, 0.0), axis=0, keepdims=True) for oh in onehots]
    denom = s_rows[0]
    for s in s_rows[1:]:
        denom = denom + s
    for kk, oh in enumerate(onehots):
        e_ref[kk:kk + 1, :] = picks[kk]
        r_ref[kk:kk + 1, :] = jnp.sum(jnp.where(oh, rank, 0.0), axis=0, keepdims=True).astype(jnp.int32)
        w_ref[kk:kk + 1, :] = s_rows[kk] / denom * ROUTED_SCALE
    e_ref[TOP_K:SLOTS, :] = jnp.zeros((SLOTS - TOP_K, tt), jnp.int32)
    r_ref[TOP_K:SLOTS, :] = jnp.zeros((SLOTS - TOP_K, tt), jnp.int32)
    w_ref[TOP_K:SLOTS, :] = jnp.zeros((SLOTS - TOP_K, tt), F32)


def _router(h1, router_w, router_b):
    tokens, d_model = h1.shape
    n_exp = router_w.shape[1]
    tt = ROUTER_TILE
    assert tokens % tt == 0 and n_exp <= LANES and n_exp % SUBLANES == 0
    rw_pad = jnp.pad(router_w, ((0, 0), (0, LANES - n_exp)))
    rb_cols = jnp.broadcast_to(router_b.reshape(n_exp, 1), (n_exp, tt))
    tab = lambda dt: jax.ShapeDtypeStruct((SLOTS, tokens), dt)
    tab_spec = pl.BlockSpec((SLOTS, tt), lambda t: (0, t))
    return pl.pallas_call(
        _router_kernel,
        grid=(tokens // tt,),
        in_specs=[pl.BlockSpec((tt, d_model), lambda t: (t, 0)),
                  pl.BlockSpec((d_model, LANES), lambda t: (0, 0)),
                  pl.BlockSpec((n_exp, tt), lambda t: (0, 0))],
        out_specs=[tab_spec, tab_spec, tab_spec, pl.BlockSpec((n_exp, LANES), lambda t: (0, 0))],
        out_shape=[tab(jnp.int32), tab(jnp.int32), tab(F32), jax.ShapeDtypeStruct((n_exp, LANES), F32)],
        compiler_params=pltpu.CompilerParams(dimension_semantics=("arbitrary",), vmem_limit_bytes=VMEM_LIMIT),
        name="router",
    )(h1, rw_pad, rb_cols)


def _positions_kernel(start_ref, e_ref, r_ref, pos_ref):
    e = e_ref[...]
    pos = r_ref[...]
    for ex in range(start_ref.shape[0]):
        pos = pos + jnp.where(e == ex, start_ref[ex], 0)
    pos_ref[...] = pos


def _positions(pad_starts, e_tab, r_tab):
    slots, tokens = e_tab.shape
    tt = POSITION_TILE
    assert tokens % tt == 0
    spec = pl.BlockSpec((slots, tt), lambda t, st: (0, t))
    grid_spec = pltpu.PrefetchScalarGridSpec(
        num_scalar_prefetch=1, grid=(tokens // tt,), in_specs=[spec, spec], out_specs=spec)
    return pl.pallas_call(
        _positions_kernel,
        grid_spec=grid_spec,
        out_shape=jax.ShapeDtypeStruct((slots, tokens), jnp.int32),
        compiler_params=pltpu.CompilerParams(dimension_semantics=("arbitrary",), vmem_limit_bytes=VMEM_LIMIT),
        name="positions",
    )(pad_starts, e_tab, r_tab)


def _dispatch_kernel(tail_ref, pos_ref, h_ref, xs_ref, zbuf, sem, zsem):
    tt = h_ref.shape[0]
    n_exp = tail_ref.shape[0]

    @pl.when(pl.program_id(0) == 0)
    def _():
        zbuf[...] = jnp.zeros_like(zbuf)

        def tail_copy(e):
            start = pl.multiple_of(tail_ref[e], zbuf.shape[0])
            return pltpu.make_async_copy(zbuf, xs_ref.at[pl.ds(start, zbuf.shape[0]), :], zsem)

        def zstart(e, carry):
            tail_copy(e).start()
            return carry

        def zwait(e, carry):
            tail_copy(e).wait()
            return carry

        lax.fori_loop(0, n_exp, zstart, 0)
        lax.fori_loop(0, n_exp, zwait, 0)

    def issue(t, carry):
        for kk in range(TOP_K):
            p = pos_ref[t * SLOTS + kk]
            pltpu.make_async_copy(h_ref.at[pl.ds(t, 1), :], xs_ref.at[pl.ds(p, 1), :], sem).start(priority=kk % 2)
        return carry

    lax.fori_loop(0, tt, issue, 0)
    for kk in range(TOP_K):
        pltpu.make_async_copy(h_ref, xs_ref.at[pl.ds(0, tt), :], sem).wait()


def _dispatch(tails, pos_flat, h1, rows_total):
    tokens, d_model = h1.shape
    tt = DISPATCH_TILE
    assert tokens % tt == 0
    grid_spec = pltpu.PrefetchScalarGridSpec(
        num_scalar_prefetch=1,
        grid=(tokens // tt,),
        in_specs=[pl.BlockSpec((tt * SLOTS,), lambda t, tl: (t,), memory_space=pltpu.SMEM),
                  pl.BlockSpec((tt, d_model), lambda t, tl: (t, 0))],
        out_specs=pl.BlockSpec(memory_space=pl.ANY),
        scratch_shapes=[pltpu.VMEM((EXPERT_ROWS, d_model), F32),
                        pltpu.SemaphoreType.DMA, pltpu.SemaphoreType.DMA],
    )
    return pl.pallas_call(
        _dispatch_kernel,
        grid_spec=grid_spec,
        out_shape=jax.ShapeDtypeStruct((rows_total, d_model), F32),
        compiler_params=pltpu.CompilerParams(dimension_semantics=("arbitrary",), vmem_limit_bytes=VMEM_LIMIT),
        name="dispatch",
    )(tails, pos_flat, h1)


def _expert_kernel(be_ref, na_ref, xs_ref, wg_ref, wu_ref, wd_ref, ys_ref, wgu_bf, wd_bf):
    j = pl.program_id(0)
    d_exp = wd_ref.shape[0]

    @pl.when(jnp.logical_or(j == 0, be_ref[j] != be_ref[jnp.maximum(j - 1, 0)]))
    def _():
        wgu_bf[:, :d_exp] = wg_ref[...].astype(BF16)
        wgu_bf[:, d_exp:] = wu_ref[...].astype(BF16)
        wd_bf[...] = wd_ref[...].astype(BF16)

    @pl.when(j < na_ref[0])
    def _():
        gu = _dot(xs_ref[...].astype(BF16), wgu_bf[...])
        act = _silu(gu[:, :d_exp]) * gu[:, d_exp:]
        ys_ref[...] = _dot(act.astype(BF16), wd_bf[...])


def _experts(block_expert, n_active, xs, w_gate, w_up, w_down):
    rows_total, d_model = xs.shape
    eb = EXPERT_ROWS
    n_blocks = rows_total // eb
    d_exp = w_down.shape[1]

    def row_map(j, be, na):
        return (jnp.minimum(j, na[0] - 1), 0)

    def w_map(j, be, na):
        return (be[j], 0, 0)

    grid_spec = pltpu.PrefetchScalarGridSpec(
        num_scalar_prefetch=2,
        grid=(n_blocks,),
        in_specs=[pl.BlockSpec((eb, d_model), row_map),
                  pl.BlockSpec((None, d_model, d_exp), w_map),
                  pl.BlockSpec((None, d_model, d_exp), w_map),
                  pl.BlockSpec((None, d_exp, d_model), w_map)],
        out_specs=pl.BlockSpec((eb, d_model), row_map),
        scratch_shapes=[pltpu.VMEM((d_model, 2 * d_exp), BF16), pltpu.VMEM((d_exp, d_model), BF16)],
    )
    return pl.pallas_call(
        _expert_kernel,
        grid_spec=grid_spec,
        out_shape=jax.ShapeDtypeStruct((rows_total, d_model), F32),
        compiler_params=pltpu.CompilerParams(dimension_semantics=("arbitrary",), vmem_limit_bytes=VMEM_LIMIT),
        name="experts",
    )(block_expert, n_active, xs, w_gate, w_up, w_down)


def _combine_kernel(pos_ref, nxt_ref, h_ref, w_ref, wsgu_ref, wsd_ref, l2g_ref, l2b_ref, ys_ref, out_ref,
                    gbuf, sems):
    step = pl.program_id(0)
    n_steps = pl.num_programs(0)
    tt = h_ref.shape[0]
    d_sh = wsd_ref.shape[0]
    cur = step % 2

    def gather(table_ref, buf):
        def issue(t, carry):
            for kk in range(TOP_K):
                p = table_ref[t * SLOTS + kk]
                pltpu.make_async_copy(ys_ref.at[pl.ds(p, 1), :], gbuf.at[buf, kk, pl.ds(t, 1), :],
                                      sems.at[buf]).start(priority=kk % 2)
            return carry

        lax.fori_loop(0, tt, issue, 0)

    @pl.when(step == 0)
    def _():
        gather(pos_ref, 0)

    @pl.when(step + 1 < n_steps)
    def _():
        gather(nxt_ref, 1 - cur)

    h = h_ref[...]
    gu = _dot(h.astype(BF16), wsgu_ref[...])
    act = _silu(gu[:, :d_sh]) * gu[:, d_sh:]
    acc = DEEPNORM_ALPHA * h + _dot(act.astype(BF16), wsd_ref[...])

    for kk in range(TOP_K):
        pltpu.make_async_copy(ys_ref.at[pl.ds(0, tt), :], gbuf.at[cur, kk], sems.at[cur]).wait()

    w = w_ref[...]
    for kk in range(TOP_K):
        acc = acc + w[:, kk:kk + 1] * gbuf[cur, kk]
    out_ref[...] = _layer_norm(acc, l2g_ref[...], l2b_ref[...])


def _combine(pos_flat, h1, w_tok, ws_gu, ws_down, l2g, l2b, ys):
    tokens, d_model = h1.shape
    tt = COMBINE_TILE
    assert tokens % tt == 0
    full = lambda a: pl.BlockSpec(a.shape, lambda t: (0,) * a.ndim)
    n_tiles = tokens // tt
    return pl.pallas_call(
        _combine_kernel,
        grid=(n_tiles,),
        in_specs=[pl.BlockSpec((tt * SLOTS,), lambda t: (t,), memory_space=pltpu.SMEM),
                  pl.BlockSpec((tt * SLOTS,), lambda t: (jnp.minimum(t + 1, n_tiles - 1),),
                               memory_space=pltpu.SMEM),
                  pl.BlockSpec((tt, d_model), lambda t: (t, 0)),
                  pl.BlockSpec((tt, SLOTS), lambda t: (t, 0)),
                  full(ws_gu), full(ws_down), full(l2g), full(l2b),
                  pl.BlockSpec(memory_space=pl.ANY)],
        out_specs=pl.BlockSpec((tt, d_model), lambda t: (t, 0)),
        out_shape=jax.ShapeDtypeStruct((tokens, d_model), F32),
        scratch_shapes=[pltpu.VMEM((2, TOP_K, tt, d_model), F32), pltpu.SemaphoreType.DMA((2,))],
        compiler_params=pltpu.CompilerParams(dimension_semantics=("arbitrary",), vmem_limit_bytes=VMEM_LIMIT),
        name="combine",
    )(pos_flat, pos_flat, h1, w_tok, ws_gu, ws_down, l2g, l2b, ys)


def kernel(x, meta_tokens, ln_in_g, ln_in_b, w_in, b_in, hg_lb_logits, hg_norm_g, conv_w, conv_b, lru_wa, lru_ba,
           lru_wx, lru_bx, lru_lambda, w_out, ln1_g, ln1_b, router_w, router_bias, we_gate, we_up, we_down,
           ws_gate, ws_up, ws_down, ln2_g, ln2_b):
    batch, seq, d_model = x.shape
    assert w_in.shape[0] == DEPTH
    n_exp = router_w.shape[-1]
    row = lambda a: a.reshape(1, -1).astype(F32)

    meta_pad = jnp.concatenate([jnp.zeros((ROW_BLOCK - N_META, d_model), F32), meta_tokens.astype(F32)], axis=0)
    w_lru = jnp.concatenate([lru_wa[0], lru_wx[0]], axis=-1).astype(BF16)
    h1 = _mixer(x, meta_pad, row(ln_in_g), row(ln_in_b), w_in[0].astype(BF16), row(b_in[0]),
                hg_lb_logits.astype(F32), row(hg_norm_g[0]), conv_w[0].astype(F32), row(conv_b[0]), w_lru,
                row(lru_ba[0]), row(lru_bx[0]), row(lru_lambda[0]), w_out[0].astype(BF16),
                row(ln1_g[0]), row(ln1_b[0]))
    tokens = batch * seq
    h1 = h1.reshape(tokens, d_model)

    e_tab, r_tab, w_tab, counts = _router(h1, router_w[0].astype(F32), router_bias[0].astype(F32))

    eb = EXPERT_ROWS
    n_blocks = tokens * TOP_K // eb + n_exp
    counts = counts[:, 0].astype(jnp.int32)
    padded = (counts + eb - 1) // eb * eb
    pad_ends = jnp.cumsum(padded)
    pad_starts = pad_ends - padded
    n_active = (pad_ends[-1] // eb).astype(jnp.int32)
    blk = jnp.minimum(jnp.arange(n_blocks, dtype=jnp.int32), n_active - 1)
    block_expert = jnp.sum((pad_ends[None, :] <= (blk * eb)[:, None]).astype(jnp.int32), axis=1)
    block_expert = jnp.minimum(block_expert, n_exp - 1)
    pos = _positions(pad_starts.astype(jnp.int32), e_tab, r_tab)
    pos_flat = pos.T.reshape(tokens * SLOTS).astype(jnp.int32)
    tails = jnp.maximum(pad_ends - eb, 0).astype(jnp.int32)

    xs = _dispatch(tails, pos_flat, h1, n_blocks * eb)
    ys = _experts(block_expert, n_active.reshape(1), xs, we_gate[0].astype(F32), we_up[0].astype(F32),
                  we_down[0].astype(F32))
    ws_gu = jnp.concatenate([ws_gate[0], ws_up[0]], axis=-1).astype(BF16)
    out = _combine(pos_flat, h1, w_tab.T, ws_gu, ws_down[0].astype(BF16), row(ln2_g[0]), row(ln2_b[0]), ys)
    return out.reshape(batch, seq, d_model)
```

```python
import jax
import jax.numpy as jnp
from jax import lax
from jax.experimental import pallas as pl
from jax.experimental.pallas import tpu as pltpu

F32 = jnp.float32
BF16 = jnp.bfloat16

N_META = 16
LN_EPS = 1e-5
RMS_EPS = 1e-6
HG_HEADS = 8
HEAD_DIM = 128
LRU_BLOCKS = 8
LRU_BLOCK = 128
CONV_W = 4
LRU_C = 8.0
TOP_K = 6
ROUTED_SCALE = 2.5
DEPTH = 1
DEEPNORM_ALPHA = (2.0 * DEPTH) ** 0.25
LOG2_E = 1.4426950408889634

ROW_BLOCK = 256
CHUNK = 128
ROUTER_TILE = 512
POSITION_TILE = 1024
DISPATCH_TILE = 512
COMBINE_TILE = 256
EXPERT_ROWS = 512
SLOTS = 8
SUBLANES = 8
LANES = 128
VMEM_LIMIT = 56 * 1024 * 1024


def _layer_norm(x, g, b):
    mu = jnp.mean(x, axis=-1, keepdims=True)
    xc = x - mu
    var = jnp.mean(xc * xc, axis=-1, keepdims=True)
    return xc * lax.rsqrt(var + LN_EPS) * g + b


def _sigmoid(x):
    return 1.0 / (1.0 + jnp.exp2(x * (-LOG2_E)))


def _silu(x):
    return x * _sigmoid(x)


def _gelu_tanh(x):
    return 0.5 * x * (1.0 + jnp.tanh(0.7978845608028654 * (x + 0.044715 * (x * x * x))))


def _dot(a, b):
    return jnp.dot(a, b, preferred_element_type=F32)


def _dot_nt(a, b):
    return lax.dot_general(a, b, (((1,), (1,)), ((), ())), preferred_element_type=F32)


def _split3(x):
    hi = x.astype(BF16)
    r1 = x - hi.astype(F32)
    mid = r1.astype(BF16)
    lo = (r1 - mid.astype(F32)).astype(BF16)
    return hi, mid, lo


def _linear_scan(a, u, h0):
    rows, width = a.shape
    groups = rows // SUBLANES
    a3 = a.reshape(groups, SUBLANES, width)
    u3 = u.reshape(groups, SUBLANES, width)
    sid = lax.broadcasted_iota(jnp.int32, (groups, SUBLANES, width), 1)
    d = 1
    while d < SUBLANES:
        keep = sid >= d
        a_sh = jnp.where(keep, pltpu.roll(a3, d, 1), 1.0)
        u_sh = jnp.where(keep, pltpu.roll(u3, d, 1), 0.0)
        u3 = u3 + a3 * u_sh
        a3 = a3 * a_sh
        d *= 2
    out = []
    carry = h0
    for g in range(groups):
        hg = u3[g] + a3[g] * carry
        out.append(hg)
        carry = hg[SUBLANES - 1:SUBLANES, :]
    return jnp.concatenate(out, axis=0)


def _mixer_kernel(x_ref, meta_ref, lng_ref, lnb_ref, win_ref, bin_ref, lbl_ref, hgn_ref, cw_ref, cb_ref,
                  wlru_ref, ba_ref, bx_ref, lam_ref, wout_ref, l1g_ref, l1b_ref,
                  h1_ref,
                  st_ref, cwin_ref, hcar_ref, yhg_ref, st0_ref, cwin0_ref, hcar0_ref):
    bi = pl.program_id(0)
    i = pl.program_id(1)
    is_first = i == 0

    @pl.when(jnp.logical_and(is_first, bi == 0))
    def _():
        st_ref[...] = jnp.zeros_like(st_ref)
        cwin_ref[0:SUBLANES, :] = jnp.zeros_like(cwin0_ref)
        hcar_ref[...] = jnp.zeros_like(hcar_ref)

    @pl.when(jnp.logical_and(is_first, bi > 0))
    def _():
        st_ref[...] = st0_ref[...]
        cwin_ref[0:SUBLANES, :] = cwin0_ref[...]
        hcar_ref[...] = hcar0_ref[...]

    @pl.when(jnp.logical_or(jnp.logical_not(is_first), bi == 0))
    def _():
        _mixer_block(x_ref, meta_ref, lng_ref, lnb_ref, win_ref, bin_ref, lbl_ref, hgn_ref, cw_ref, cb_ref,
                     wlru_ref, ba_ref, bx_ref, lam_ref, wout_ref, l1g_ref, l1b_ref, h1_ref,
                     st_ref, cwin_ref, hcar_ref, yhg_ref, is_first)

        @pl.when(is_first)
        def _():
            st0_ref[...] = st_ref[...]
            cwin0_ref[...] = cwin_ref[0:SUBLANES, :]
            hcar0_ref[...] = hcar_ref[...]


def _mixer_block(x_ref, meta_ref, lng_ref, lnb_ref, win_ref, bin_ref, lbl_ref, hgn_ref, cw_ref, cb_ref,
                 wlru_ref, ba_ref, bx_ref, lam_ref, wout_ref, l1g_ref, l1b_ref, h1_ref,
                 st_ref, cwin_ref, hcar_ref, yhg_ref, is_first):
    rows = x_ref.shape[0]
    d_model = x_ref.shape[1]
    rid = lax.broadcasted_iota(jnp.int32, (rows, 1), 0)
    valid = jnp.logical_or(rid >= rows - N_META, jnp.logical_not(is_first))
    xin = jnp.where(is_first, meta_ref[...], x_ref[...])
    h = _layer_norm(xin, lng_ref[...], lnb_ref[...])
    hb = h.astype(BF16)

    def proj(seg):
        cols = slice(seg * d_model, (seg + 1) * d_model)
        return _dot(hb, win_ref[:, cols]) + bin_ref[:, cols]

    lbl = lbl_ref[...]
    lmax = jnp.max(lbl, axis=0, keepdims=True)
    lexp = jnp.exp(lbl - lmax)
    lb = lexp[0:1, :] / jnp.sum(lexp, axis=0, keepdims=True)

    q = _silu(proj(0))
    f = lb + (1.0 - lb) * _sigmoid(proj(1))
    lf = jnp.where(valid, jnp.log(f), 0.0)
    k = jnp.where(valid, 1.0 - f, 0.0)
    v = proj(2)

    r_i = lax.broadcasted_iota(jnp.int32, (CHUNK, CHUNK), 0)
    c_i = lax.broadcasted_iota(jnp.int32, (CHUNK, CHUNK), 1)
    causal = r_i >= c_i
    tri = jnp.where(causal, 1.0, 0.0).astype(BF16)
    hgn = hgn_ref[...]
    half = CHUNK // 2
    for c in range(rows // CHUNK):
        rs = slice(c * CHUNK, (c + 1) * CHUNK)
        lf_hi, lf_mid, lf_lo = _split3(lf[rs])
        b = _dot(tri, lf_hi) + _dot(tri, lf_mid) + _dot(tri, lf_lo)
        b_mid = b[half - 1:half, :]
        b_last = b[CHUNK - 1:CHUNK, :]
        q_t = q[rs] * jnp.exp(b - b_mid)
        k_t = k[rs] * jnp.exp(b_mid - b)
        q_s = (q_t * jnp.exp(b_mid)).astype(BF16)
        k_s = (k_t * jnp.exp(b_last - b_mid)).astype(BF16)
        q_t = q_t.astype(BF16)
        k_t = k_t.astype(BF16)
        s_decay = jnp.exp(b_last)
        vc = v[rs]
        vb = vc.astype(BF16)
        for hd in range(HG_HEADS):
            cs = slice(hd * HEAD_DIM, (hd + 1) * HEAD_DIM)
            p = jnp.where(causal, _dot_nt(q_t[:, cs], k_t[:, cs]), 0.0)
            st = st_ref[hd]
            o = _dot(p.astype(BF16), vb[:, cs]) + _dot_nt(q_s[:, cs], st.astype(BF16))
            st_ref[hd] = st * s_decay[:, cs] + _dot(vc[:, cs].T.astype(BF16), k_s[:, cs])
            o = o * lax.rsqrt(jnp.mean(o * o, axis=-1, keepdims=True) + RMS_EPS) * hgn
            yhg_ref[rs, cs] = o
    y_hg = yhg_ref[...] * _silu(proj(3))

    lx = jnp.where(valid, proj(4), 0.0)
    cwin_ref[SUBLANES:SUBLANES + rows, :] = lx
    cw = cw_ref[...]
    first_tap = SUBLANES - (CONV_W - 1)
    xc = cb_ref[...] + cw[0:1, :] * cwin_ref[first_tap:first_tap + rows, :]
    for j in range(1, CONV_W):
        xc = xc + cw[j:j + 1, :] * cwin_ref[first_tap + j:first_tap + j + rows, :]
    cwin_ref[0:SUBLANES, :] = lx[rows - SUBLANES:rows, :]

    ra_parts, ri_parts = [], []
    for n in range(LRU_BLOCKS):
        cs = slice(n * LRU_BLOCK, (n + 1) * LRU_BLOCK)
        g2 = _dot(xc[:, cs].astype(BF16), wlru_ref[n])
        ra_parts.append(g2[:, :LRU_BLOCK])
        ri_parts.append(g2[:, LRU_BLOCK:])
    r_gate = _sigmoid(jnp.concatenate(ra_parts, axis=1) + ba_ref[...])
    i_gate = _sigmoid(jnp.concatenate(ri_parts, axis=1) + bx_ref[...])
    neg = -lam_ref[...]
    softplus_neg = jnp.maximum(neg, 0.0) + jnp.log(1.0 + jnp.exp(-jnp.abs(neg)))
    rate = (-LRU_C * LOG2_E) * softplus_neg
    a = jnp.exp2(r_gate * rate)
    u = jnp.where(valid, jnp.sqrt(1.0 - a * a) * (i_gate * xc), 0.0)
    h_lru = _linear_scan(a, u, hcar_ref[0:1, :])
    hcar_ref[0:1, :] = h_lru[rows - 1:rows, :]
    y_lru = h_lru * _gelu_tanh(proj(5))

    merged = _sigmoid(proj(6)) * y_hg + _sigmoid(proj(7)) * y_lru
    mix = _dot(merged.astype(BF16), wout_ref[...])
    h1_ref[...] = _layer_norm(DEEPNORM_ALPHA * h + mix, l1g_ref[...], l1b_ref[...])


def _const_spec(shape):
    nd = len(shape)
    return pl.BlockSpec(shape, lambda b, i: (0,) * nd, pipeline_mode=pl.Buffered(1))


def _mixer(x, meta_pad, ln_g, ln_b, w_in, b_in, lb_logits, hg_norm, conv_w, conv_b, w_lru, ba, bx, lam,
           w_out, l1g, l1b):
    batch, seq, d_model = x.shape
    rows = ROW_BLOCK
    assert seq % rows == 0 and d_model == HG_HEADS * HEAD_DIM == LRU_BLOCKS * LRU_BLOCK
    n_blk = seq // rows + 1
    consts = (meta_pad, ln_g, ln_b, w_in, b_in, lb_logits, hg_norm, conv_w, conv_b, w_lru, ba, bx, lam,
              w_out, l1g, l1b)
    x_spec = pl.BlockSpec((None, rows, d_model), lambda b, i: (b, jnp.maximum(i - 1, 0), 0))
    return pl.pallas_call(
        _mixer_kernel,
        grid=(batch, n_blk),
        in_specs=[x_spec] + [_const_spec(c.shape) for c in consts],
        out_specs=pl.BlockSpec((None, rows, d_model), lambda b, i: (b, jnp.maximum(i - 1, 0), 0)),
        out_shape=jax.ShapeDtypeStruct((batch, seq, d_model), F32),
        scratch_shapes=[
            pltpu.VMEM((HG_HEADS, HEAD_DIM, HEAD_DIM), F32),
            pltpu.VMEM((rows + SUBLANES, d_model), F32),
            pltpu.VMEM((SUBLANES, d_model), F32),
            pltpu.VMEM((rows, d_model), F32),
            pltpu.VMEM((HG_HEADS, HEAD_DIM, HEAD_DIM), F32),
            pltpu.VMEM((SUBLANES, d_model), F32),
            pltpu.VMEM((SUBLANES, d_model), F32),
        ],
        compiler_params=pltpu.CompilerParams(
            dimension_semantics=("arbitrary", "arbitrary"), vmem_limit_bytes=VMEM_LIMIT),
        name="mixer",
    )(x, *consts)


def _router_kernel(h_ref, rw_ref, rb_ref, e_ref, r_ref, w_ref, cnt_ref):
    step = pl.program_id(0)

    @pl.when(step == 0)
    def _():
        cnt_ref[...] = jnp.zeros_like(cnt_ref)

    h = h_ref[...]
    tt = h.shape[0]
    n_exp = cnt_ref.shape[0]
    h_hi = h.astype(BF16)
    h_lo = (h - h_hi.astype(F32)).astype(BF16)
    rw = rw_ref[...]
    w_hi = rw.astype(BF16)
    w_lo = (rw - w_hi.astype(F32)).astype(BF16)
    logits = _dot(h_hi, w_hi) + _dot(h_hi, w_lo) + _dot(h_lo, w_hi)
    scores = _sigmoid(logits.T[0:n_exp, :])
    sel = scores + rb_ref[...]
    eid = lax.broadcasted_iota(jnp.int32, (n_exp, tt), 0)
    onehots, picks = [], []
    for _ in range(TOP_K):
        m = jnp.max(sel, axis=0, keepdims=True)
        idx = jnp.min(jnp.where(sel == m, eid, n_exp), axis=0, keepdims=True)
        oh = eid == idx
        onehots.append(oh)
        picks.append(idx)
        sel = jnp.where(oh, -jnp.inf, sel)
    chosen = onehots[0]
    for oh in onehots[1:]:
        chosen = jnp.logical_or(chosen, oh)
    chosen_f = jnp.where(chosen, 1.0, 0.0)
    r_i = lax.broadcasted_iota(jnp.int32, (tt, tt), 0)
    c_i = lax.broadcasted_iota(jnp.int32, (tt, tt), 1)
    earlier = jnp.where(r_i < c_i, 1.0, 0.0).astype(BF16)
    counts = cnt_ref[...]
    rank = _dot(chosen_f.astype(BF16), earlier) + counts[:, 0:1]
    cnt_ref[...] = counts + jnp.sum(chosen_f, axis=1, keepdims=True)

    s_rows = [jnp.sum(jnp.where(oh, scores, 0.0), axis=0, keepdims=True) for oh in onehots]
    denom = s_rows[0]
    for s in s_rows[1:]:
        denom = denom + s
    for kk, oh in enumerate(onehots):
        e_ref[kk:kk + 1, :] = picks[kk]
        r_ref[kk:kk + 1, :] = jnp.sum(jnp.where(oh, rank, 0.0), axis=0, keepdims=True).astype(jnp.int32)
        w_ref[kk:kk + 1, :] = s_rows[kk] / denom * ROUTED_SCALE
    e_ref[TOP_K:SLOTS, :] = jnp.zeros((SLOTS - TOP_K, tt), jnp.int32)
    r_ref[TOP_K:SLOTS, :] = jnp.zeros((SLOTS - TOP_K, tt), jnp.int32)
    w_ref[TOP_K:SLOTS, :] = jnp.zeros((SLOTS - TOP_K, tt), F32)


def _router(h1, router_w, router_b):
    tokens, d_model = h1.shape
    n_exp = router_w.shape[1]
    tt = ROUTER_TILE
    assert tokens % tt == 0 and n_exp <= LANES and n_exp % SUBLANES == 0
    rw_pad = jnp.pad(router_w, ((0, 0), (0, LANES - n_exp)))
    rb_cols = jnp.broadcast_to(router_b.reshape(n_exp, 1), (n_exp, tt))
    tab = lambda dt: jax.ShapeDtypeStruct((SLOTS, tokens), dt)
    tab_spec = pl.BlockSpec((SLOTS, tt), lambda t: (0, t))
    return pl.pallas_call(
        _router_kernel,
        grid=(tokens // tt,),
        in_specs=[pl.BlockSpec((tt, d_model), lambda t: (t, 0)),
                  pl.BlockSpec((d_model, LANES), lambda t: (0, 0)),
                  pl.BlockSpec((n_exp, tt), lambda t: (0, 0))],
        out_specs=[tab_spec, tab_spec, tab_spec, pl.BlockSpec((n_exp, LANES), lambda t: (0, 0))],
        out_shape=[tab(jnp.int32), tab(jnp.int32), tab(F32), jax.ShapeDtypeStruct((n_exp, LANES), F32)],
        compiler_params=pltpu.CompilerParams(dimension_semantics=("arbitrary",), vmem_limit_bytes=VMEM_LIMIT),
        name="router",
    )(h1, rw_pad, rb_cols)


def _positions_kernel(start_ref, e_ref, r_ref, pos_ref):
    e = e_ref[...]
    pos = r_ref[...]
    for ex in range(start_ref.shape[0]):
        pos = pos + jnp.where(e == ex, start_ref[ex], 0)
    pos_ref[...] = pos


def _positions(pad_starts, e_tab, r_tab):
    slots, tokens = e_tab.shape
    tt = POSITION_TILE
    assert tokens % tt == 0
    spec = pl.BlockSpec((slots, tt), lambda t, st: (0, t))
    grid_spec = pltpu.PrefetchScalarGridSpec(
        num_scalar_prefetch=1, grid=(tokens // tt,), in_specs=[spec, spec], out_specs=spec)
    return pl.pallas_call(
        _positions_kernel,
        grid_spec=grid_spec,
        out_shape=jax.ShapeDtypeStruct((slots, tokens), jnp.int32),
        compiler_params=pltpu.CompilerParams(dimension_semantics=("arbitrary",), vmem_limit_bytes=VMEM_LIMIT),
        name="positions",
    )(pad_starts, e_tab, r_tab)


def _to_tiles(ref, x):
    rows = x.shape[0]
    for s in range(SUBLANES):
        ref[pl.ds(s, rows, stride=SUBLANES), :] = x[:, s * LANES:(s + 1) * LANES]


def _from_tiles(ref, start, rows):
    return jnp.concatenate([ref[pl.ds(start + s, rows, stride=SUBLANES), :] for s in range(SUBLANES)], axis=1)


def _tile_rows(idx):
    return pl.ds(pl.multiple_of(idx * SUBLANES, SUBLANES), SUBLANES)


def _dispatch_kernel(tail_ref, pos_ref, h_ref, xs_ref, zbuf, src, sem, zsem):
    tt = h_ref.shape[0]
    n_exp = tail_ref.shape[0]
    fill = zbuf.shape[0]

    @pl.when(pl.program_id(0) == 0)
    def _():
        zbuf[...] = jnp.zeros_like(zbuf)

        def tail_copy(e):
            start = pl.multiple_of(tail_ref[e] * SUBLANES, fill)
            return pltpu.make_async_copy(zbuf, xs_ref.at[pl.ds(start, fill), :], zsem)

        def zstart(e, carry):
            tail_copy(e).start()
            return carry

        def zwait(e, carry):
            tail_copy(e).wait()
            return carry

        lax.fori_loop(0, n_exp, zstart, 0)
        lax.fori_loop(0, n_exp, zwait, 0)

    _to_tiles(src, h_ref[...])

    def issue(t, carry):
        for kk in range(TOP_K):
            p = pos_ref[t * SLOTS + kk]
            pltpu.make_async_copy(src.at[_tile_rows(t), :], xs_ref.at[_tile_rows(p), :], sem).start(
                priority=kk % 2)
        return carry

    lax.fori_loop(0, tt, issue, 0)
    for kk in range(TOP_K):
        pltpu.make_async_copy(src, xs_ref.at[pl.ds(0, tt * SUBLANES), :], sem).wait()


def _dispatch(tails, pos_flat, h1, rows_total):
    tokens, d_model = h1.shape
    tt = DISPATCH_TILE
    assert tokens % tt == 0 and d_model == SUBLANES * LANES
    grid_spec = pltpu.PrefetchScalarGridSpec(
        num_scalar_prefetch=1,
        grid=(tokens // tt,),
        in_specs=[pl.BlockSpec((tt * SLOTS,), lambda t, tl: (t,), memory_space=pltpu.SMEM),
                  pl.BlockSpec((tt, d_model), lambda t, tl: (t, 0))],
        out_specs=pl.BlockSpec(memory_space=pl.ANY),
        scratch_shapes=[pltpu.VMEM((EXPERT_ROWS * SUBLANES, LANES), F32),
                        pltpu.VMEM((tt * SUBLANES, LANES), F32),
                        pltpu.SemaphoreType.DMA, pltpu.SemaphoreType.DMA],
    )
    return pl.pallas_call(
        _dispatch_kernel,
        grid_spec=grid_spec,
        out_shape=jax.ShapeDtypeStruct((rows_total * SUBLANES, LANES), F32),
        compiler_params=pltpu.CompilerParams(dimension_semantics=("arbitrary",), vmem_limit_bytes=VMEM_LIMIT),
        name="dispatch",
    )(tails, pos_flat, h1)


def _expert_kernel(be_ref, na_ref, xs_ref, wg_ref, wu_ref, wd_ref, ys_ref, wgu_bf, wd_bf):
    j = pl.program_id(0)
    d_exp = wd_ref.shape[0]

    @pl.when(jnp.logical_or(j == 0, be_ref[j] != be_ref[jnp.maximum(j - 1, 0)]))
    def _():
        wgu_bf[:, :d_exp] = wg_ref[...].astype(BF16)
        wgu_bf[:, d_exp:] = wu_ref[...].astype(BF16)
        wd_bf[...] = wd_ref[...].astype(BF16)

    @pl.when(j < na_ref[0])
    def _():
        rows = xs_ref.shape[0] // SUBLANES
        gu = _dot(_from_tiles(xs_ref, 0, rows).astype(BF16), wgu_bf[...])
        act = _silu(gu[:, :d_exp]) * gu[:, d_exp:]
        _to_tiles(ys_ref, _dot(act.astype(BF16), wd_bf[...]))


def _experts(block_expert, n_active, xs, w_gate, w_up, w_down):
    rows_total = xs.shape[0] // SUBLANES
    d_model = w_down.shape[2]
    eb = EXPERT_ROWS
    n_blocks = rows_total // eb
    d_exp = w_down.shape[1]

    def row_map(j, be, na):
        return (jnp.minimum(j, na[0] - 1), 0)

    def w_map(j, be, na):
        return (be[j], 0, 0)

    grid_spec = pltpu.PrefetchScalarGridSpec(
        num_scalar_prefetch=2,
        grid=(n_blocks,),
        in_specs=[pl.BlockSpec((eb * SUBLANES, LANES), row_map),
                  pl.BlockSpec((None, d_model, d_exp), w_map),
                  pl.BlockSpec((None, d_model, d_exp), w_map),
                  pl.BlockSpec((None, d_exp, d_model), w_map)],
        out_specs=pl.BlockSpec((eb * SUBLANES, LANES), row_map),
        scratch_shapes=[pltpu.VMEM((d_model, 2 * d_exp), BF16), pltpu.VMEM((d_exp, d_model), BF16)],
    )
    return pl.pallas_call(
        _expert_kernel,
        grid_spec=grid_spec,
        out_shape=jax.ShapeDtypeStruct(xs.shape, F32),
        compiler_params=pltpu.CompilerParams(dimension_semantics=("arbitrary",), vmem_limit_bytes=VMEM_LIMIT),
        name="experts",
    )(block_expert, n_active, xs, w_gate, w_up, w_down)


def _combine_kernel(pos_ref, nxt_ref, h_ref, w_ref, wsgu_ref, wsd_ref, l2g_ref, l2b_ref, ys_ref, out_ref,
                    gbuf, sems):
    step = pl.program_id(0)
    n_steps = pl.num_programs(0)
    tt = h_ref.shape[0]
    d_sh = wsd_ref.shape[0]
    cur = step % 2

    def plane(buf, kk):
        return (buf * TOP_K + kk) * tt

    def gather(table_ref, buf):
        def issue(t, carry):
            for kk in range(TOP_K):
                p = table_ref[t * SLOTS + kk]
                pltpu.make_async_copy(ys_ref.at[_tile_rows(p), :], gbuf.at[_tile_rows(plane(buf, kk) + t), :],
                                      sems.at[buf]).start(priority=kk % 2)
            return carry

        lax.fori_loop(0, tt, issue, 0)

    @pl.when(step == 0)
    def _():
        gather(pos_ref, 0)

    @pl.when(step + 1 < n_steps)
    def _():
        gather(nxt_ref, 1 - cur)

    h = h_ref[...]
    gu = _dot(h.astype(BF16), wsgu_ref[...])
    act = _silu(gu[:, :d_sh]) * gu[:, d_sh:]
    acc = DEEPNORM_ALPHA * h + _dot(act.astype(BF16), wsd_ref[...])

    for kk in range(TOP_K):
        first = pl.multiple_of(plane(cur, kk) * SUBLANES, SUBLANES)
        pltpu.make_async_copy(ys_ref.at[pl.ds(0, tt * SUBLANES), :], gbuf.at[pl.ds(first, tt * SUBLANES), :],
                              sems.at[cur]).wait()

    w = w_ref[...]
    for kk in range(TOP_K):
        acc = acc + w[:, kk:kk + 1] * _from_tiles(gbuf, plane(cur, kk) * SUBLANES, tt)
    out_ref[...] = _layer_norm(acc, l2g_ref[...], l2b_ref[...])


def _combine(pos_flat, h1, w_tok, ws_gu, ws_down, l2g, l2b, ys):
    tokens, d_model = h1.shape
    tt = COMBINE_TILE
    assert tokens % tt == 0
    full = lambda a: pl.BlockSpec(a.shape, lambda t: (0,) * a.ndim)
    n_tiles = tokens // tt
    return pl.pallas_call(
        _combine_kernel,
        grid=(n_tiles,),
        in_specs=[pl.BlockSpec((tt * SLOTS,), lambda t: (t,), memory_space=pltpu.SMEM),
                  pl.BlockSpec((tt * SLOTS,), lambda t: (jnp.minimum(t + 1, n_tiles - 1),),
                               memory_space=pltpu.SMEM),
                  pl.BlockSpec((tt, d_model), lambda t: (t, 0)),
                  pl.BlockSpec((tt, SLOTS), lambda t: (t, 0)),
                  full(ws_gu), full(ws_down), full(l2g), full(l2b),
                  pl.BlockSpec(memory_space=pl.ANY)],
        out_specs=pl.BlockSpec((tt, d_model), lambda t: (t, 0)),
        out_shape=jax.ShapeDtypeStruct((tokens, d_model), F32),
        scratch_shapes=[pltpu.VMEM((2 * TOP_K * tt * SUBLANES, LANES), F32), pltpu.SemaphoreType.DMA((2,))],
        compiler_params=pltpu.CompilerParams(dimension_semantics=("arbitrary",), vmem_limit_bytes=VMEM_LIMIT),
        name="combine",
    )(pos_flat, pos_flat, h1, w_tok, ws_gu, ws_down, l2g, l2b, ys)


def kernel(x, meta_tokens, ln_in_g, ln_in_b, w_in, b_in, hg_lb_logits, hg_norm_g, conv_w, conv_b, lru_wa, lru_ba,
           lru_wx, lru_bx, lru_lambda, w_out, ln1_g, ln1_b, router_w, router_bias, we_gate, we_up, we_down,
           ws_gate, ws_up, ws_down, ln2_g, ln2_b):
    batch, seq, d_model = x.shape
    assert w_in.shape[0] == DEPTH
    n_exp = router_w.shape[-1]
    row = lambda a: a.reshape(1, -1).astype(F32)

    meta_pad = jnp.concatenate([jnp.zeros((ROW_BLOCK - N_META, d_model), F32), meta_tokens.astype(F32)], axis=0)
    w_lru = jnp.concatenate([lru_wa[0], lru_wx[0]], axis=-1).astype(BF16)
    h1 = _mixer(x, meta_pad, row(ln_in_g), row(ln_in_b), w_in[0].astype(BF16), row(b_in[0]),
                hg_lb_logits.astype(F32), row(hg_norm_g[0]), conv_w[0].astype(F32), row(conv_b[0]), w_lru,
                row(lru_ba[0]), row(lru_bx[0]), row(lru_lambda[0]), w_out[0].astype(BF16),
                row(ln1_g[0]), row(ln1_b[0]))
    tokens = batch * seq
    h1 = h1.reshape(tokens, d_model)

    e_tab, r_tab, w_tab, counts = _router(h1, router_w[0].astype(F32), router_bias[0].astype(F32))

    eb = EXPERT_ROWS
    n_blocks = tokens * TOP_K // eb + n_exp
    counts = counts[:, 0].astype(jnp.int32)
    padded = (counts + eb - 1) // eb * eb
    pad_ends = jnp.cumsum(padded)
    pad_starts = pad_ends - padded
    n_active = (pad_ends[-1] // eb).astype(jnp.int32)
    blk = jnp.minimum(jnp.arange(n_blocks, dtype=jnp.int32), n_active - 1)
    block_expert = jnp.sum((pad_ends[None, :] <= (blk * eb)[:, None]).astype(jnp.int32), axis=1)
    block_expert = jnp.minimum(block_expert, n_exp - 1)
    pos = _positions(pad_starts.astype(jnp.int32), e_tab, r_tab)
    pos_flat = pos.T.reshape(tokens * SLOTS).astype(jnp.int32)
    tails = jnp.maximum(pad_ends - eb, 0).astype(jnp.int32)

    xs = _dispatch(tails, pos_flat, h1, n_blocks * eb)
    ys = _experts(block_expert, n_active.reshape(1), xs, we_gate[0].astype(F32), we_up[0].astype(F32),
                  we_down[0].astype(F32))
    ws_gu = jnp.concatenate([ws_gate[0], ws_up[0]], axis=-1).astype(BF16)
    out = _combine(pos_flat, h1, w_tab.T, ws_gu, ws_down[0].astype(BF16), row(ln2_g[0]), row(ln2_b[0]), ys)
    return out.reshape(batch, seq, d_model)
```

```python
import jax
import jax.numpy as jnp
from jax import lax
from jax.experimental import pallas as pl
from jax.experimental.pallas import tpu as pltpu

F32 = jnp.float32
BF16 = jnp.bfloat16

N_META = 16
LN_EPS = 1e-5
RMS_EPS = 1e-6
HG_HEADS = 8
HEAD_DIM = 128
LRU_BLOCKS = 8
LRU_BLOCK = 128
CONV_W = 4
LRU_C = 8.0
TOP_K = 6
ROUTED_SCALE = 2.5
DEPTH = 1
DEEPNORM_ALPHA = (2.0 * DEPTH) ** 0.25
LOG2_E = 1.4426950408889634

ROW_BLOCK = 256
CHUNK = 128
ROUTER_TILE = 512
POSITION_TILE = 1024
DISPATCH_TILE = 512
COMBINE_TILE = 256
EXPERT_ROWS = 512
SLOTS = 8
SUBLANES = 8
LANES = 128
VMEM_LIMIT = 56 * 1024 * 1024


def _layer_norm(x, g, b):
    mu = jnp.mean(x, axis=-1, keepdims=True)
    xc = x - mu
    var = jnp.mean(xc * xc, axis=-1, keepdims=True)
    return xc * lax.rsqrt(var + LN_EPS) * g + b


def _sigmoid(x):
    return 1.0 / (1.0 + jnp.exp2(x * (-LOG2_E)))


def _silu(x):
    return x * _sigmoid(x)


def _gelu_tanh(x):
    return 0.5 * x * (1.0 + jnp.tanh(0.7978845608028654 * (x + 0.044715 * (x * x * x))))


def _dot(a, b):
    return jnp.dot(a, b, preferred_element_type=F32)


def _dot_nt(a, b):
    return lax.dot_general(a, b, (((1,), (1,)), ((), ())), preferred_element_type=F32)


def _split3(x):
    hi = x.astype(BF16)
    r1 = x - hi.astype(F32)
    mid = r1.astype(BF16)
    lo = (r1 - mid.astype(F32)).astype(BF16)
    return hi, mid, lo


def _linear_scan(a, u, h0):
    rows, width = a.shape
    groups = rows // SUBLANES
    a3 = a.reshape(groups, SUBLANES, width)
    u3 = u.reshape(groups, SUBLANES, width)
    sid = lax.broadcasted_iota(jnp.int32, (groups, SUBLANES, width), 1)
    d = 1
    while d < SUBLANES:
        keep = sid >= d
        a_sh = jnp.where(keep, pltpu.roll(a3, d, 1), 1.0)
        u_sh = jnp.where(keep, pltpu.roll(u3, d, 1), 0.0)
        u3 = u3 + a3 * u_sh
        a3 = a3 * a_sh
        d *= 2
    out = []
    carry = h0
    for g in range(groups):
        hg = u3[g] + a3[g] * carry
        out.append(hg)
        carry = hg[SUBLANES - 1:SUBLANES, :]
    return jnp.concatenate(out, axis=0)


def _mixer_kernel(x_ref, meta_ref, lng_ref, lnb_ref, win_hbm, bin_ref, lbl_ref, hgn_ref, cw_ref, cb_ref,
                  wlru_hbm, ba_ref, bx_ref, lam_ref, wout_hbm, l1g_ref, l1b_ref,
                  h1_ref,
                  st_ref, cwin_ref, hcar_ref, yhg_ref, st0_ref, cwin0_ref, hcar0_ref,
                  win_ref, wlru_ref, wout_ref, stage, stage_lru, wsem):
    bi = pl.program_id(0)
    i = pl.program_id(1)
    is_first = i == 0

    @pl.when(jnp.logical_and(is_first, bi == 0))
    def _():
        st_ref[...] = jnp.zeros_like(st_ref)
        cwin_ref[0:SUBLANES, :] = jnp.zeros_like(cwin0_ref)
        hcar_ref[...] = jnp.zeros_like(hcar_ref)

        d_model = stage.shape[0]

        def fetch(src, dst):
            cp = pltpu.make_async_copy(src, dst, wsem)
            cp.start()
            cp.wait()

        for c in range(win_ref.shape[1] // d_model):
            cols = slice(c * d_model, (c + 1) * d_model)
            fetch(win_hbm.at[:, cols], stage)
            win_ref[:, cols] = stage[...].astype(BF16)
        fetch(wout_hbm, stage)
        wout_ref[...] = stage[...].astype(BF16)
        fetch(wlru_hbm, stage_lru)
        wlru_ref[...] = stage_lru[...].astype(BF16)

    @pl.when(jnp.logical_and(is_first, bi > 0))
    def _():
        st_ref[...] = st0_ref[...]
        cwin_ref[0:SUBLANES, :] = cwin0_ref[...]
        hcar_ref[...] = hcar0_ref[...]

    @pl.when(jnp.logical_or(jnp.logical_not(is_first), bi == 0))
    def _():
        _mixer_block(x_ref, meta_ref, lng_ref, lnb_ref, win_ref, bin_ref, lbl_ref, hgn_ref, cw_ref, cb_ref,
                     wlru_ref, ba_ref, bx_ref, lam_ref, wout_ref, l1g_ref, l1b_ref, h1_ref,
                     st_ref, cwin_ref, hcar_ref, yhg_ref, is_first)

        @pl.when(is_first)
        def _():
            st0_ref[...] = st_ref[...]
            cwin0_ref[...] = cwin_ref[0:SUBLANES, :]
            hcar0_ref[...] = hcar_ref[...]


def _mixer_block(x_ref, meta_ref, lng_ref, lnb_ref, win_ref, bin_ref, lbl_ref, hgn_ref, cw_ref, cb_ref,
                 wlru_ref, ba_ref, bx_ref, lam_ref, wout_ref, l1g_ref, l1b_ref, h1_ref,
                 st_ref, cwin_ref, hcar_ref, yhg_ref, is_first):
    rows = x_ref.shape[0]
    d_model = x_ref.shape[1]
    rid = lax.broadcasted_iota(jnp.int32, (rows, 1), 0)
    valid = jnp.logical_or(rid >= rows - N_META, jnp.logical_not(is_first))
    xin = jnp.where(is_first, meta_ref[...], x_ref[...])
    h = _layer_norm(xin, lng_ref[...], lnb_ref[...])
    hb = h.astype(BF16)

    def proj(seg):
        cols = slice(seg * d_model, (seg + 1) * d_model)
        return _dot(hb, win_ref[:, cols]) + bin_ref[:, cols]

    lbl = lbl_ref[...]
    lmax = jnp.max(lbl, axis=0, keepdims=True)
    lexp = jnp.exp(lbl - lmax)
    lb = lexp[0:1, :] / jnp.sum(lexp, axis=0, keepdims=True)

    q = _silu(proj(0))
    f = lb + (1.0 - lb) * _sigmoid(proj(1))
    lf = jnp.where(valid, jnp.log(f), 0.0)
    k = jnp.where(valid, 1.0 - f, 0.0)
    v = proj(2)

    r_i = lax.broadcasted_iota(jnp.int32, (CHUNK, CHUNK), 0)
    c_i = lax.broadcasted_iota(jnp.int32, (CHUNK, CHUNK), 1)
    causal = r_i >= c_i
    tri = jnp.where(causal, 1.0, 0.0).astype(BF16)
    hgn = hgn_ref[...]
    half = CHUNK // 2
    for c in range(rows // CHUNK):
        rs = slice(c * CHUNK, (c + 1) * CHUNK)
        lf_hi, lf_mid, lf_lo = _split3(lf[rs])
        b = _dot(tri, lf_hi) + _dot(tri, lf_mid) + _dot(tri, lf_lo)
        b_mid = b[half - 1:half, :]
        b_last = b[CHUNK - 1:CHUNK, :]
        q_t = q[rs] * jnp.exp(b - b_mid)
        k_t = k[rs] * jnp.exp(b_mid - b)
        q_s = (q_t * jnp.exp(b_mid)).astype(BF16)
        k_s = (k_t * jnp.exp(b_last - b_mid)).astype(BF16)
        q_t = q_t.astype(BF16)
        k_t = k_t.astype(BF16)
        s_decay = jnp.exp(b_last)
        vc = v[rs]
        vb = vc.astype(BF16)
        for hd in range(HG_HEADS):
            cs = slice(hd * HEAD_DIM, (hd + 1) * HEAD_DIM)
            p = jnp.where(causal, _dot_nt(q_t[:, cs], k_t[:, cs]), 0.0)
            st = st_ref[hd]
            o = _dot(p.astype(BF16), vb[:, cs]) + _dot_nt(q_s[:, cs], st.astype(BF16))
            st_ref[hd] = st * s_decay[:, cs] + _dot(vc[:, cs].T.astype(BF16), k_s[:, cs])
            o = o * lax.rsqrt(jnp.mean(o * o, axis=-1, keepdims=True) + RMS_EPS) * hgn
            yhg_ref[rs, cs] = o
    y_hg = yhg_ref[...] * _silu(proj(3))

    lx = jnp.where(valid, proj(4), 0.0)
    cwin_ref[SUBLANES:SUBLANES + rows, :] = lx
    cw = cw_ref[...]
    first_tap = SUBLANES - (CONV_W - 1)
    xc = cb_ref[...] + cw[0:1, :] * cwin_ref[first_tap:first_tap + rows, :]
    for j in range(1, CONV_W):
        xc = xc + cw[j:j + 1, :] * cwin_ref[first_tap + j:first_tap + j + rows, :]
    cwin_ref[0:SUBLANES, :] = lx[rows - SUBLANES:rows, :]

    ra_parts, ri_parts = [], []
    for n in range(LRU_BLOCKS):
        cs = slice(n * LRU_BLOCK, (n + 1) * LRU_BLOCK)
        g2 = _dot(xc[:, cs].astype(BF16), wlru_ref[n])
        ra_parts.append(g2[:, :LRU_BLOCK])
        ri_parts.append(g2[:, LRU_BLOCK:])
    r_gate = _sigmoid(jnp.concatenate(ra_parts, axis=1) + ba_ref[...])
    i_gate = _sigmoid(jnp.concatenate(ri_parts, axis=1) + bx_ref[...])
    neg = -lam_ref[...]
    softplus_neg = jnp.maximum(neg, 0.0) + jnp.log(1.0 + jnp.exp(-jnp.abs(neg)))
    rate = (-LRU_C * LOG2_E) * softplus_neg
    a = jnp.exp2(r_gate * rate)
    u = jnp.where(valid, jnp.sqrt(1.0 - a * a) * (i_gate * xc), 0.0)
    h_lru = _linear_scan(a, u, hcar_ref[0:1, :])
    hcar_ref[0:1, :] = h_lru[rows - 1:rows, :]
    y_lru = h_lru * _gelu_tanh(proj(5))

    merged = _sigmoid(proj(6)) * y_hg + _sigmoid(proj(7)) * y_lru
    mix = _dot(merged.astype(BF16), wout_ref[...])
    h1_ref[...] = _layer_norm(DEEPNORM_ALPHA * h + mix, l1g_ref[...], l1b_ref[...])


def _const_spec(shape):
    nd = len(shape)
    return pl.BlockSpec(shape, lambda b, i: (0,) * nd, pipeline_mode=pl.Buffered(1))


def _mixer(x, meta_pad, ln_g, ln_b, w_in, b_in, lb_logits, hg_norm, conv_w, conv_b, w_lru, ba, bx, lam,
           w_out, l1g, l1b):
    batch, seq, d_model = x.shape
    rows = ROW_BLOCK
    assert seq % rows == 0 and d_model == HG_HEADS * HEAD_DIM == LRU_BLOCKS * LRU_BLOCK
    n_blk = seq // rows + 1
    consts = (meta_pad, ln_g, ln_b, w_in, b_in, lb_logits, hg_norm, conv_w, conv_b, w_lru, ba, bx, lam,
              w_out, l1g, l1b)
    big = (w_in, w_lru, w_out)
    assert w_out.shape == (d_model, d_model) and w_in.shape[1] % d_model == 0
    x_spec = pl.BlockSpec((None, rows, d_model), lambda b, i: (b, jnp.maximum(i - 1, 0), 0))
    return pl.pallas_call(
        _mixer_kernel,
        grid=(batch, n_blk),
        in_specs=[x_spec] + [pl.BlockSpec(memory_space=pl.ANY) if any(c is w for w in big)
                             else _const_spec(c.shape) for c in consts],
        out_specs=pl.BlockSpec((None, rows, d_model), lambda b, i: (b, jnp.maximum(i - 1, 0), 0)),
        out_shape=jax.ShapeDtypeStruct((batch, seq, d_model), F32),
        scratch_shapes=[
            pltpu.VMEM((HG_HEADS, HEAD_DIM, HEAD_DIM), F32),
            pltpu.VMEM((rows + SUBLANES, d_model), F32),
            pltpu.VMEM((SUBLANES, d_model), F32),
            pltpu.VMEM((rows, d_model), F32),
            pltpu.VMEM((HG_HEADS, HEAD_DIM, HEAD_DIM), F32),
            pltpu.VMEM((SUBLANES, d_model), F32),
            pltpu.VMEM((SUBLANES, d_model), F32),
            pltpu.VMEM(w_in.shape, BF16),
            pltpu.VMEM(w_lru.shape, BF16),
            pltpu.VMEM(w_out.shape, BF16),
            pltpu.VMEM((d_model, d_model), F32),
            pltpu.VMEM(w_lru.shape, F32),
            pltpu.SemaphoreType.DMA,
        ],
        compiler_params=pltpu.CompilerParams(
            dimension_semantics=("arbitrary", "arbitrary"), vmem_limit_bytes=VMEM_LIMIT),
        name="mixer",
    )(x, *consts)


def _router_kernel(h_ref, rw_ref, rb_ref, e_ref, r_ref, w_ref, cnt_ref):
    step = pl.program_id(0)

    @pl.when(step == 0)
    def _():
        cnt_ref[...] = jnp.zeros_like(cnt_ref)

    h = h_ref[...]
    tt = h.shape[0]
    n_exp = cnt_ref.shape[0]
    h_hi = h.astype(BF16)
    h_lo = (h - h_hi.astype(F32)).astype(BF16)
    rw = rw_ref[...]
    w_hi = rw.astype(BF16)
    w_lo = (rw - w_hi.astype(F32)).astype(BF16)
    logits = _dot(h_hi, w_hi) + _dot(h_hi, w_lo) + _dot(h_lo, w_hi)
    scores = _sigmoid(logits.T[0:n_exp, :])
    sel = scores + rb_ref[...]
    eid = lax.broadcasted_iota(jnp.int32, (n_exp, tt), 0)
    onehots, picks = [], []
    for _ in range(TOP_K):
        m = jnp.max(sel, axis=0, keepdims=True)
        idx = jnp.min(jnp.where(sel == m, eid, n_exp), axis=0, keepdims=True)
        oh = eid == idx
        onehots.append(oh)
        picks.append(idx)
        sel = jnp.where(oh, -jnp.inf, sel)
    chosen = onehots[0]
    for oh in onehots[1:]:
        chosen = jnp.logical_or(chosen, oh)
    chosen_f = jnp.where(chosen, 1.0, 0.0)
    r_i = lax.broadcasted_iota(jnp.int32, (tt, tt), 0)
    c_i = lax.broadcasted_iota(jnp.int32, (tt, tt), 1)
    earlier = jnp.where(r_i < c_i, 1.0, 0.0).astype(BF16)
    counts = cnt_ref[...]
    rank = _dot(chosen_f.astype(BF16), earlier) + counts[:, 0:1]
    cnt_ref[...] = counts + jnp.sum(chosen_f, axis=1, keepdims=True)

    s_rows = [jnp.sum(jnp.where(oh, scores, 0.0), axis=0, keepdims=True) for oh in onehots]
    denom = s_rows[0]
    for s in s_rows[1:]:
        denom = denom + s
    for kk, oh in enumerate(onehots):
        e_ref[kk:kk + 1, :] = picks[kk]
        r_ref[kk:kk + 1, :] = jnp.sum(jnp.where(oh, rank, 0.0), axis=0, keepdims=True).astype(jnp.int32)
        w_ref[kk:kk + 1, :] = s_rows[kk] / denom * ROUTED_SCALE
    e_ref[TOP_K:SLOTS, :] = jnp.zeros((SLOTS - TOP_K, tt), jnp.int32)
    r_ref[TOP_K:SLOTS, :] = jnp.zeros((SLOTS - TOP_K, tt), jnp.int32)
    w_ref[TOP_K:SLOTS, :] = jnp.zeros((SLOTS - TOP_K, tt), F32)


def _router(h1, router_w, router_b):
    tokens, d_model = h1.shape
    n_exp = router_w.shape[1]
    tt = ROUTER_TILE
    assert tokens % tt == 0 and n_exp <= LANES and n_exp % SUBLANES == 0
    rw_pad = jnp.pad(router_w, ((0, 0), (0, LANES - n_exp)))
    rb_cols = jnp.broadcast_to(router_b.reshape(n_exp, 1), (n_exp, tt))
    tab = lambda dt: jax.ShapeDtypeStruct((SLOTS, tokens), dt)
    tab_spec = pl.BlockSpec((SLOTS, tt), lambda t: (0, t))
    return pl.pallas_call(
        _router_kernel,
        grid=(tokens // tt,),
        in_specs=[pl.BlockSpec((tt, d_model), lambda t: (t, 0)),
                  pl.BlockSpec((d_model, LANES), lambda t: (0, 0)),
                  pl.BlockSpec((n_exp, tt), lambda t: (0, 0))],
        out_specs=[tab_spec, tab_spec, tab_spec, pl.BlockSpec((n_exp, LANES), lambda t: (0, 0))],
        out_shape=[tab(jnp.int32), tab(jnp.int32), tab(F32), jax.ShapeDtypeStruct((n_exp, LANES), F32)],
        compiler_params=pltpu.CompilerParams(dimension_semantics=("arbitrary",), vmem_limit_bytes=VMEM_LIMIT),
        name="router",
    )(h1, rw_pad, rb_cols)


def _positions_kernel(start_ref, e_ref, r_ref, pos_ref):
    e = e_ref[...]
    pos = r_ref[...]
    for ex in range(start_ref.shape[0]):
        pos = pos + jnp.where(e == ex, start_ref[ex], 0)
    pos_ref[...] = pos


def _positions(pad_starts, e_tab, r_tab):
    slots, tokens = e_tab.shape
    tt = POSITION_TILE
    assert tokens % tt == 0
    spec = pl.BlockSpec((slots, tt), lambda t, st: (0, t))
    grid_spec = pltpu.PrefetchScalarGridSpec(
        num_scalar_prefetch=1, grid=(tokens // tt,), in_specs=[spec, spec], out_specs=spec)
    return pl.pallas_call(
        _positions_kernel,
        grid_spec=grid_spec,
        out_shape=jax.ShapeDtypeStruct((slots, tokens), jnp.int32),
        compiler_params=pltpu.CompilerParams(dimension_semantics=("arbitrary",), vmem_limit_bytes=VMEM_LIMIT),
        name="positions",
    )(pad_starts, e_tab, r_tab)


def _to_tiles(ref, x):
    rows = x.shape[0]
    for s in range(SUBLANES):
        ref[pl.ds(s, rows, stride=SUBLANES), :] = x[:, s * LANES:(s + 1) * LANES]


def _from_tiles(ref, start, rows):
    return jnp.concatenate([ref[pl.ds(start + s, rows, stride=SUBLANES), :] for s in range(SUBLANES)], axis=1)


def _tile_rows(idx):
    return pl.ds(pl.multiple_of(idx * SUBLANES, SUBLANES), SUBLANES)


def _dispatch_kernel(tail_ref, pos_ref, h_ref, xs_ref, zbuf, src, sem, zsem):
    tt = h_ref.shape[0]
    n_exp = tail_ref.shape[0]
    fill = zbuf.shape[0]

    @pl.when(pl.program_id(0) == 0)
    def _():
        zbuf[...] = jnp.zeros_like(zbuf)

        def tail_copy(e):
            start = pl.multiple_of(tail_ref[e] * SUBLANES, fill)
            return pltpu.make_async_copy(zbuf, xs_ref.at[pl.ds(start, fill), :], zsem)

        def zstart(e, carry):
            tail_copy(e).start()
            return carry

        def zwait(e, carry):
            tail_copy(e).wait()
            return carry

        lax.fori_loop(0, n_exp, zstart, 0)
        lax.fori_loop(0, n_exp, zwait, 0)

    _to_tiles(src, h_ref[...])

    def issue(t, carry):
        for kk in range(TOP_K):
            p = pos_ref[t * SLOTS + kk]
            pltpu.make_async_copy(src.at[_tile_rows(t), :], xs_ref.at[_tile_rows(p), :], sem).start(
                priority=kk % 2)
        return carry

    lax.fori_loop(0, tt, issue, 0)
    for kk in range(TOP_K):
        pltpu.make_async_copy(src, xs_ref.at[pl.ds(0, tt * SUBLANES), :], sem).wait()


def _dispatch(tails, pos_flat, h1, rows_total):
    tokens, d_model = h1.shape
    tt = DISPATCH_TILE
    assert tokens % tt == 0 and d_model == SUBLANES * LANES
    grid_spec = pltpu.PrefetchScalarGridSpec(
        num_scalar_prefetch=1,
        grid=(tokens // tt,),
        in_specs=[pl.BlockSpec((tt * SLOTS,), lambda t, tl: (t,), memory_space=pltpu.SMEM),
                  pl.BlockSpec((tt, d_model), lambda t, tl: (t, 0))],
        out_specs=pl.BlockSpec(memory_space=pl.ANY),
        scratch_shapes=[pltpu.VMEM((EXPERT_ROWS * SUBLANES, LANES), F32),
                        pltpu.VMEM((tt * SUBLANES, LANES), F32),
                        pltpu.SemaphoreType.DMA, pltpu.SemaphoreType.DMA],
    )
    return pl.pallas_call(
        _dispatch_kernel,
        grid_spec=grid_spec,
        out_shape=jax.ShapeDtypeStruct((rows_total * SUBLANES, LANES), F32),
        compiler_params=pltpu.CompilerParams(dimension_semantics=("arbitrary",), vmem_limit_bytes=VMEM_LIMIT),
        name="dispatch",
    )(tails, pos_flat, h1)


def _expert_kernel(be_ref, na_ref, xs_ref, wg_ref, wu_ref, wd_ref, ys_ref, wgu_bf, wd_bf):
    j = pl.program_id(0)
    d_exp = wd_ref.shape[0]

    @pl.when(jnp.logical_or(j == 0, be_ref[j] != be_ref[jnp.maximum(j - 1, 0)]))
    def _():
        wgu_bf[:, :d_exp] = wg_ref[...].astype(BF16)
        wgu_bf[:, d_exp:] = wu_ref[...].astype(BF16)
        wd_bf[...] = wd_ref[...].astype(BF16)

    @pl.when(j < na_ref[0])
    def _():
        rows = xs_ref.shape[0] // SUBLANES
        gu = _dot(_from_tiles(xs_ref, 0, rows).astype(BF16), wgu_bf[...])
        act = _silu(gu[:, :d_exp]) * gu[:, d_exp:]
        _to_tiles(ys_ref, _dot(act.astype(BF16), wd_bf[...]))


def _experts(block_expert, n_active, xs, w_gate, w_up, w_down):
    rows_total = xs.shape[0] // SUBLANES
    d_model = w_down.shape[2]
    eb = EXPERT_ROWS
    n_blocks = rows_total // eb
    d_exp = w_down.shape[1]

    def row_map(j, be, na):
        return (jnp.minimum(j, na[0] - 1), 0)

    def w_map(j, be, na):
        return (be[j], 0, 0)

    grid_spec = pltpu.PrefetchScalarGridSpec(
        num_scalar_prefetch=2,
        grid=(n_blocks,),
        in_specs=[pl.BlockSpec((eb * SUBLANES, LANES), row_map),
                  pl.BlockSpec((None, d_model, d_exp), w_map),
                  pl.BlockSpec((None, d_model, d_exp), w_map),
                  pl.BlockSpec((None, d_exp, d_model), w_map)],
        out_specs=pl.BlockSpec((eb * SUBLANES, LANES), row_map),
        scratch_shapes=[pltpu.VMEM((d_model, 2 * d_exp), BF16), pltpu.VMEM((d_exp, d_model), BF16)],
    )
    return pl.pallas_call(
        _expert_kernel,
        grid_spec=grid_spec,
        out_shape=jax.ShapeDtypeStruct(xs.shape, F32),
        compiler_params=pltpu.CompilerParams(dimension_semantics=("arbitrary",), vmem_limit_bytes=VMEM_LIMIT),
        name="experts",
    )(block_expert, n_active, xs, w_gate, w_up, w_down)


def _combine_kernel(pos_ref, nxt_ref, h_ref, w_ref, wsgu_ref, wsd_ref, l2g_ref, l2b_ref, ys_ref, out_ref,
                    gbuf, sems):
    step = pl.program_id(0)
    n_steps = pl.num_programs(0)
    tt = h_ref.shape[0]
    d_sh = wsd_ref.shape[0]
    cur = step % 2

    def plane(buf, kk):
        return (buf * TOP_K + kk) * tt

    def gather(table_ref, buf):
        def issue(t, carry):
            for kk in range(TOP_K):
                p = table_ref[t * SLOTS + kk]
                pltpu.make_async_copy(ys_ref.at[_tile_rows(p), :], gbuf.at[_tile_rows(plane(buf, kk) + t), :],
                                      sems.at[buf]).start(priority=kk % 2)
            return carry

        lax.fori_loop(0, tt, issue, 0)

    @pl.when(step == 0)
    def _():
        gather(pos_ref, 0)

    @pl.when(step + 1 < n_steps)
    def _():
        gather(nxt_ref, 1 - cur)

    h = h_ref[...]
    gu = _dot(h.astype(BF16), wsgu_ref[...])
    act = _silu(gu[:, :d_sh]) * gu[:, d_sh:]
    acc = DEEPNORM_ALPHA * h + _dot(act.astype(BF16), wsd_ref[...])

    for kk in range(TOP_K):
        first = pl.multiple_of(plane(cur, kk) * SUBLANES, SUBLANES)
        pltpu.make_async_copy(ys_ref.at[pl.ds(0, tt * SUBLANES), :], gbuf.at[pl.ds(first, tt * SUBLANES), :],
                              sems.at[cur]).wait()

    w = w_ref[...]
    for kk in range(TOP_K):
        acc = acc + w[:, kk:kk + 1] * _from_tiles(gbuf, plane(cur, kk) * SUBLANES, tt)
    out_ref[...] = _layer_norm(acc, l2g_ref[...], l2b_ref[...])


def _combine(pos_flat, h1, w_tok, ws_gu, ws_down, l2g, l2b, ys):
    tokens, d_model = h1.shape
    tt = COMBINE_TILE
    assert tokens % tt == 0
    full = lambda a: pl.BlockSpec(a.shape, lambda t: (0,) * a.ndim)
    n_tiles = tokens // tt
    return pl.pallas_call(
        _combine_kernel,
        grid=(n_tiles,),
        in_specs=[pl.BlockSpec((tt * SLOTS,), lambda t: (t,), memory_space=pltpu.SMEM),
                  pl.BlockSpec((tt * SLOTS,), lambda t: (jnp.minimum(t + 1, n_tiles - 1),),
                               memory_space=pltpu.SMEM),
                  pl.BlockSpec((tt, d_model), lambda t: (t, 0)),
                  pl.BlockSpec((tt, SLOTS), lambda t: (t, 0)),
                  full(ws_gu), full(ws_down), full(l2g), full(l2b),
                  pl.BlockSpec(memory_space=pl.ANY)],
        out_specs=pl.BlockSpec((tt, d_model), lambda t: (t, 0)),
        out_shape=jax.ShapeDtypeStruct((tokens, d_model), F32),
        scratch_shapes=[pltpu.VMEM((2 * TOP_K * tt * SUBLANES, LANES), F32), pltpu.SemaphoreType.DMA((2,))],
        compiler_params=pltpu.CompilerParams(dimension_semantics=("arbitrary",), vmem_limit_bytes=VMEM_LIMIT),
        name="combine",
    )(pos_flat, pos_flat, h1, w_tok, ws_gu, ws_down, l2g, l2b, ys)


def kernel(x, meta_tokens, ln_in_g, ln_in_b, w_in, b_in, hg_lb_logits, hg_norm_g, conv_w, conv_b, lru_wa, lru_ba,
           lru_wx, lru_bx, lru_lambda, w_out, ln1_g, ln1_b, router_w, router_bias, we_gate, we_up, we_down,
           ws_gate, ws_up, ws_down, ln2_g, ln2_b):
    batch, seq, d_model = x.shape
    assert w_in.shape[0] == DEPTH
    n_exp = router_w.shape[-1]
    row = lambda a: a.reshape(1, -1).astype(F32)

    meta_pad = jnp.concatenate([jnp.zeros((ROW_BLOCK - N_META, d_model), F32), meta_tokens.astype(F32)], axis=0)
    w_lru = jnp.concatenate([lru_wa[0], lru_wx[0]], axis=-1).astype(F32)
    h1 = _mixer(x, meta_pad, row(ln_in_g), row(ln_in_b), w_in[0].astype(F32), row(b_in[0]),
                hg_lb_logits.astype(F32), row(hg_norm_g[0]), conv_w[0].astype(F32), row(conv_b[0]), w_lru,
                row(lru_ba[0]), row(lru_bx[0]), row(lru_lambda[0]), w_out[0].astype(F32),
                row(ln1_g[0]), row(ln1_b[0]))
    tokens = batch * seq
    h1 = h1.reshape(tokens, d_model)

    e_tab, r_tab, w_tab, counts = _router(h1, router_w[0].astype(F32), router_bias[0].astype(F32))

    eb = EXPERT_ROWS
    n_blocks = tokens * TOP_K // eb + n_exp
    counts = counts[:, 0].astype(jnp.int32)
    padded = (counts + eb - 1) // eb * eb
    pad_ends = jnp.cumsum(padded)
    pad_starts = pad_ends - padded
    n_active = (pad_ends[-1] // eb).astype(jnp.int32)
    blk = jnp.minimum(jnp.arange(n_blocks, dtype=jnp.int32), n_active - 1)
    block_expert = jnp.sum((pad_ends[None, :] <= (blk * eb)[:, None]).astype(jnp.int32), axis=1)
    block_expert = jnp.minimum(block_expert, n_exp - 1)
    pos = _positions(pad_starts.astype(jnp.int32), e_tab, r_tab)
    pos_flat = pos.T.reshape(tokens * SLOTS).astype(jnp.int32)
    tails = jnp.maximum(pad_ends - eb, 0).astype(jnp.int32)

    xs = _dispatch(tails, pos_flat, h1, n_blocks * eb)
    ys = _experts(block_expert, n_active.reshape(1), xs, we_gate[0].astype(F32), we_up[0].astype(F32),
                  we_down[0].astype(F32))
    ws_gu = jnp.concatenate([ws_gate[0], ws_up[0]], axis=-1).astype(BF16)
    out = _combine(pos_flat, h1, w_tab.T, ws_gu, ws_down[0].astype(BF16), row(ln2_g[0]), row(ln2_b[0]), ys)
    return out.reshape(batch, seq, d_model)
```

```python
import jax
import jax.numpy as jnp
from jax import lax
from jax.experimental import pallas as pl
from jax.experimental.pallas import tpu as pltpu

F32 = jnp.float32
BF16 = jnp.bfloat16

N_META = 16
LN_EPS = 1e-5
RMS_EPS = 1e-6
HG_HEADS = 8
HEAD_DIM = 128
LRU_BLOCKS = 8
LRU_BLOCK = 128
CONV_W = 4
LRU_C = 8.0
TOP_K = 6
ROUTED_SCALE = 2.5
DEPTH = 1
DEEPNORM_ALPHA = (2.0 * DEPTH) ** 0.25
LOG2_E = 1.4426950408889634

ROW_BLOCK = 256
CHUNK = 128
ROUTER_TILE = 512
POSITION_TILE = 1024
DISPATCH_TILE = 1024
COMBINE_TILE = 512
EXPERT_ROWS = 512
SLOTS = 8
SUBLANES = 8
LANES = 128
VMEM_LIMIT = 56 * 1024 * 1024


def _layer_norm(x, g, b):
    mu = jnp.mean(x, axis=-1, keepdims=True)
    xc = x - mu
    var = jnp.mean(xc * xc, axis=-1, keepdims=True)
    return xc * lax.rsqrt(var + LN_EPS) * g + b


def _sigmoid(x):
    return 1.0 / (1.0 + jnp.exp2(x * (-LOG2_E)))


def _silu(x):
    return x * _sigmoid(x)


def _gelu_tanh(x):
    return 0.5 * x * (1.0 + jnp.tanh(0.7978845608028654 * (x + 0.044715 * (x * x * x))))


def _dot(a, b):
    return jnp.dot(a, b, preferred_element_type=F32)


def _dot_nt(a, b):
    return lax.dot_general(a, b, (((1,), (1,)), ((), ())), preferred_element_type=F32)


def _split3(x):
    hi = x.astype(BF16)
    r1 = x - hi.astype(F32)
    mid = r1.astype(BF16)
    lo = (r1 - mid.astype(F32)).astype(BF16)
    return hi, mid, lo


def _linear_scan(a, u, h0):
    rows, width = a.shape
    groups = rows // SUBLANES
    a3 = a.reshape(groups, SUBLANES, width)
    u3 = u.reshape(groups, SUBLANES, width)
    sid = lax.broadcasted_iota(jnp.int32, (groups, SUBLANES, width), 1)
    d = 1
    while d < SUBLANES:
        keep = sid >= d
        a_sh = jnp.where(keep, pltpu.roll(a3, d, 1), 1.0)
        u_sh = jnp.where(keep, pltpu.roll(u3, d, 1), 0.0)
        u3 = u3 + a3 * u_sh
        a3 = a3 * a_sh
        d *= 2
    out = []
    carry = h0
    for g in range(groups):
        hg = u3[g] + a3[g] * carry
        out.append(hg)
        carry = hg[SUBLANES - 1:SUBLANES, :]
    return jnp.concatenate(out, axis=0)


def _mixer_kernel(x_ref, meta_ref, lng_ref, lnb_ref, win_ref, bin_ref, lbl_ref, hgn_ref, cw_ref, cb_ref,
                  wlru_ref, ba_ref, bx_ref, lam_ref, wout_ref, l1g_ref, l1b_ref,
                  h1_ref,
                  st_ref, cwin_ref, hcar_ref, yhg_ref, st0_ref, cwin0_ref, hcar0_ref):
    bi = pl.program_id(0)
    i = pl.program_id(1)
    is_first = i == 0

    @pl.when(jnp.logical_and(is_first, bi == 0))
    def _():
        st_ref[...] = jnp.zeros_like(st_ref)
        cwin_ref[0:SUBLANES, :] = jnp.zeros_like(cwin0_ref)
        hcar_ref[...] = jnp.zeros_like(hcar_ref)

    @pl.when(jnp.logical_and(is_first, bi > 0))
    def _():
        st_ref[...] = st0_ref[...]
        cwin_ref[0:SUBLANES, :] = cwin0_ref[...]
        hcar_ref[...] = hcar0_ref[...]

    @pl.when(jnp.logical_or(jnp.logical_not(is_first), bi == 0))
    def _():
        _mixer_block(x_ref, meta_ref, lng_ref, lnb_ref, win_ref, bin_ref, lbl_ref, hgn_ref, cw_ref, cb_ref,
                     wlru_ref, ba_ref, bx_ref, lam_ref, wout_ref, l1g_ref, l1b_ref, h1_ref,
                     st_ref, cwin_ref, hcar_ref, yhg_ref, is_first)

        @pl.when(is_first)
        def _():
            st0_ref[...] = st_ref[...]
            cwin0_ref[...] = cwin_ref[0:SUBLANES, :]
            hcar0_ref[...] = hcar_ref[...]


def _mixer_block(x_ref, meta_ref, lng_ref, lnb_ref, win_ref, bin_ref, lbl_ref, hgn_ref, cw_ref, cb_ref,
                 wlru_ref, ba_ref, bx_ref, lam_ref, wout_ref, l1g_ref, l1b_ref, h1_ref,
                 st_ref, cwin_ref, hcar_ref, yhg_ref, is_first):
    rows = x_ref.shape[0]
    d_model = x_ref.shape[1]
    rid = lax.broadcasted_iota(jnp.int32, (rows, 1), 0)
    valid = jnp.logical_or(rid >= rows - N_META, jnp.logical_not(is_first))
    xin = jnp.where(is_first, meta_ref[...], x_ref[...])
    h = _layer_norm(xin, lng_ref[...], lnb_ref[...])
    hb = h.astype(BF16)

    def proj(seg):
        cols = slice(seg * d_model, (seg + 1) * d_model)
        return _dot(hb, win_ref[:, cols]) + bin_ref[:, cols]

    lbl = lbl_ref[...]
    lmax = jnp.max(lbl, axis=0, keepdims=True)
    lexp = jnp.exp(lbl - lmax)
    lb = lexp[0:1, :] / jnp.sum(lexp, axis=0, keepdims=True)

    q = _silu(proj(0))
    f = lb + (1.0 - lb) * _sigmoid(proj(1))
    lf = jnp.where(valid, jnp.log(f), 0.0)
    k = jnp.where(valid, 1.0 - f, 0.0)
    v = proj(2)

    r_i = lax.broadcasted_iota(jnp.int32, (CHUNK, CHUNK), 0)
    c_i = lax.broadcasted_iota(jnp.int32, (CHUNK, CHUNK), 1)
    causal = r_i >= c_i
    tri = jnp.where(causal, 1.0, 0.0).astype(BF16)
    hgn = hgn_ref[...]
    half = CHUNK // 2
    for c in range(rows // CHUNK):
        rs = slice(c * CHUNK, (c + 1) * CHUNK)
        lf_hi, lf_mid, lf_lo = _split3(lf[rs])
        b = _dot(tri, lf_hi) + _dot(tri, lf_mid) + _dot(tri, lf_lo)
        b_mid = b[half - 1:half, :]
        b_last = b[CHUNK - 1:CHUNK, :]
        q_t = q[rs] * jnp.exp(b - b_mid)
        k_t = k[rs] * jnp.exp(b_mid - b)
        q_s = (q_t * jnp.exp(b_mid)).astype(BF16)
        k_s = (k_t * jnp.exp(b_last - b_mid)).astype(BF16)
        q_t = q_t.astype(BF16)
        k_t = k_t.astype(BF16)
        s_decay = jnp.exp(b_last)
        vc = v[rs]
        vb = vc.astype(BF16)
        for hd in range(HG_HEADS):
            cs = slice(hd * HEAD_DIM, (hd + 1) * HEAD_DIM)
            p = jnp.where(causal, _dot_nt(q_t[:, cs], k_t[:, cs]), 0.0)
            st = st_ref[hd]
            o = _dot(p.astype(BF16), vb[:, cs]) + _dot_nt(q_s[:, cs], st.astype(BF16))
            st_ref[hd] = st * s_decay[:, cs] + _dot(vc[:, cs].T.astype(BF16), k_s[:, cs])
            o = o * lax.rsqrt(jnp.mean(o * o, axis=-1, keepdims=True) + RMS_EPS) * hgn
            yhg_ref[rs, cs] = o
    y_hg = yhg_ref[...] * _silu(proj(3))

    lx = jnp.where(valid, proj(4), 0.0)
    cwin_ref[SUBLANES:SUBLANES + rows, :] = lx
    cw = cw_ref[...]
    first_tap = SUBLANES - (CONV_W - 1)
    xc = cb_ref[...] + cw[0:1, :] * cwin_ref[first_tap:first_tap + rows, :]
    for j in range(1, CONV_W):
        xc = xc + cw[j:j + 1, :] * cwin_ref[first_tap + j:first_tap + j + rows, :]
    cwin_ref[0:SUBLANES, :] = lx[rows - SUBLANES:rows, :]

    ra_parts, ri_parts = [], []
    for n in range(LRU_BLOCKS):
        cs = slice(n * LRU_BLOCK, (n + 1) * LRU_BLOCK)
        g2 = _dot(xc[:, cs].astype(BF16), wlru_ref[n])
        ra_parts.append(g2[:, :LRU_BLOCK])
        ri_parts.append(g2[:, LRU_BLOCK:])
    r_gate = _sigmoid(jnp.concatenate(ra_parts, axis=1) + ba_ref[...])
    i_gate = _sigmoid(jnp.concatenate(ri_parts, axis=1) + bx_ref[...])
    neg = -lam_ref[...]
    softplus_neg = jnp.maximum(neg, 0.0) + jnp.log(1.0 + jnp.exp(-jnp.abs(neg)))
    rate = (-LRU_C * LOG2_E) * softplus_neg
    a = jnp.exp2(r_gate * rate)
    u = jnp.where(valid, jnp.sqrt(1.0 - a * a) * (i_gate * xc), 0.0)
    h_lru = _linear_scan(a, u, hcar_ref[0:1, :])
    hcar_ref[0:1, :] = h_lru[rows - 1:rows, :]
    y_lru = h_lru * _gelu_tanh(proj(5))

    merged = _sigmoid(proj(6)) * y_hg + _sigmoid(proj(7)) * y_lru
    mix = _dot(merged.astype(BF16), wout_ref[...])
    h1_ref[...] = _layer_norm(DEEPNORM_ALPHA * h + mix, l1g_ref[...], l1b_ref[...])


def _const_spec(shape):
    nd = len(shape)
    return pl.BlockSpec(shape, lambda b, i: (0,) * nd, pipeline_mode=pl.Buffered(1))


def _mixer(x, meta_pad, ln_g, ln_b, w_in, b_in, lb_logits, hg_norm, conv_w, conv_b, w_lru, ba, bx, lam,
           w_out, l1g, l1b):
    batch, seq, d_model = x.shape
    rows = ROW_BLOCK
    assert seq % rows == 0 and d_model == HG_HEADS * HEAD_DIM == LRU_BLOCKS * LRU_BLOCK
    n_blk = seq // rows + 1
    consts = (meta_pad, ln_g, ln_b, w_in, b_in, lb_logits, hg_norm, conv_w, conv_b, w_lru, ba, bx, lam,
              w_out, l1g, l1b)
    x_spec = pl.BlockSpec((None, rows, d_model), lambda b, i: (b, jnp.maximum(i - 1, 0), 0))
    return pl.pallas_call(
        _mixer_kernel,
        grid=(batch, n_blk),
        in_specs=[x_spec] + [_const_spec(c.shape) for c in consts],
        out_specs=pl.BlockSpec((None, rows, d_model), lambda b, i: (b, jnp.maximum(i - 1, 0), 0)),
        out_shape=jax.ShapeDtypeStruct((batch, seq, d_model), F32),
        scratch_shapes=[
            pltpu.VMEM((HG_HEADS, HEAD_DIM, HEAD_DIM), F32),
            pltpu.VMEM((rows + SUBLANES, d_model), F32),
            pltpu.VMEM((SUBLANES, d_model), F32),
            pltpu.VMEM((rows, d_model), F32),
            pltpu.VMEM((HG_HEADS, HEAD_DIM, HEAD_DIM), F32),
            pltpu.VMEM((SUBLANES, d_model), F32),
            pltpu.VMEM((SUBLANES, d_model), F32),
        ],
        compiler_params=pltpu.CompilerParams(
            dimension_semantics=("arbitrary", "arbitrary"), vmem_limit_bytes=VMEM_LIMIT),
        name="mixer",
    )(x, *consts)


def _router_kernel(h_ref, rw_ref, rb_ref, e_ref, r_ref, w_ref, cnt_ref):
    step = pl.program_id(0)

    @pl.when(step == 0)
    def _():
        cnt_ref[...] = jnp.zeros_like(cnt_ref)

    h = h_ref[...]
    tt = h.shape[0]
    n_exp = cnt_ref.shape[0]
    h_hi = h.astype(BF16)
    h_lo = (h - h_hi.astype(F32)).astype(BF16)
    rw = rw_ref[...]
    w_hi = rw.astype(BF16)
    w_lo = (rw - w_hi.astype(F32)).astype(BF16)
    logits = _dot(h_hi, w_hi) + _dot(h_hi, w_lo) + _dot(h_lo, w_hi)
    scores = _sigmoid(logits.T[0:n_exp, :])
    sel = scores + rb_ref[...]
    eid = lax.broadcasted_iota(jnp.int32, (n_exp, tt), 0)
    onehots, picks = [], []
    for _ in range(TOP_K):
        m = jnp.max(sel, axis=0, keepdims=True)
        idx = jnp.min(jnp.where(sel == m, eid, n_exp), axis=0, keepdims=True)
        oh = eid == idx
        onehots.append(oh)
        picks.append(idx)
        sel = jnp.where(oh, -jnp.inf, sel)
    chosen = onehots[0]
    for oh in onehots[1:]:
        chosen = jnp.logical_or(chosen, oh)
    chosen_f = jnp.where(chosen, 1.0, 0.0)
    r_i = lax.broadcasted_iota(jnp.int32, (tt, tt), 0)
    c_i = lax.broadcasted_iota(jnp.int32, (tt, tt), 1)
    earlier = jnp.where(r_i < c_i, 1.0, 0.0).astype(BF16)
    counts = cnt_ref[...]
    rank = _dot(chosen_f.astype(BF16), earlier) + counts[:, 0:1]
    cnt_ref[...] = counts + jnp.sum(chosen_f, axis=1, keepdims=True)

    s_rows = [jnp.sum(jnp.where(oh, scores, 0.0), axis=0, keepdims=True) for oh in onehots]
    denom = s_rows[0]
    for s in s_rows[1:]:
        denom = denom + s
    for kk, oh in enumerate(onehots):
        e_ref[kk:kk + 1, :] = picks[kk]
        r_ref[kk:kk + 1, :] = jnp.sum(jnp.where(oh, rank, 0.0), axis=0, keepdims=True).astype(jnp.int32)
        w_ref[kk:kk + 1, :] = s_rows[kk] / denom * ROUTED_SCALE
    e_ref[TOP_K:SLOTS, :] = jnp.zeros((SLOTS - TOP_K, tt), jnp.int32)
    r_ref[TOP_K:SLOTS, :] = jnp.zeros((SLOTS - TOP_K, tt), jnp.int32)
    w_ref[TOP_K:SLOTS, :] = jnp.zeros((SLOTS - TOP_K, tt), F32)


def _router(h1, router_w, router_b):
    tokens, d_model = h1.shape
    n_exp = router_w.shape[1]
    tt = ROUTER_TILE
    assert tokens % tt == 0 and n_exp <= LANES and n_exp % SUBLANES == 0
    rw_pad = jnp.pad(router_w, ((0, 0), (0, LANES - n_exp)))
    rb_cols = jnp.broadcast_to(router_b.reshape(n_exp, 1), (n_exp, tt))
    tab = lambda dt: jax.ShapeDtypeStruct((SLOTS, tokens), dt)
    tab_spec = pl.BlockSpec((SLOTS, tt), lambda t: (0, t))
    return pl.pallas_call(
        _router_kernel,
        grid=(tokens // tt,),
        in_specs=[pl.BlockSpec((tt, d_model), lambda t: (t, 0)),
                  pl.BlockSpec((d_model, LANES), lambda t: (0, 0)),
                  pl.BlockSpec((n_exp, tt), lambda t: (0, 0))],
        out_specs=[tab_spec, tab_spec, tab_spec, pl.BlockSpec((n_exp, LANES), lambda t: (0, 0))],
        out_shape=[tab(jnp.int32), tab(jnp.int32), tab(F32), jax.ShapeDtypeStruct((n_exp, LANES), F32)],
        compiler_params=pltpu.CompilerParams(dimension_semantics=("arbitrary",), vmem_limit_bytes=VMEM_LIMIT),
        name="router",
    )(h1, rw_pad, rb_cols)


def _positions_kernel(start_ref, e_ref, r_ref, pos_ref):
    e = e_ref[...]
    pos = r_ref[...]
    for ex in range(start_ref.shape[0]):
        pos = pos + jnp.where(e == ex, start_ref[ex], 0)
    pos_ref[...] = pos


def _positions(pad_starts, e_tab, r_tab):
    slots, tokens = e_tab.shape
    tt = POSITION_TILE
    assert tokens % tt == 0
    spec = pl.BlockSpec((slots, tt), lambda t, st: (0, t))
    grid_spec = pltpu.PrefetchScalarGridSpec(
        num_scalar_prefetch=1, grid=(tokens // tt,), in_specs=[spec, spec], out_specs=spec)
    return pl.pallas_call(
        _positions_kernel,
        grid_spec=grid_spec,
        out_shape=jax.ShapeDtypeStruct((slots, tokens), jnp.int32),
        compiler_params=pltpu.CompilerParams(dimension_semantics=("arbitrary",), vmem_limit_bytes=VMEM_LIMIT),
        name="positions",
    )(pad_starts, e_tab, r_tab)


def _to_tiles(ref, x):
    rows = x.shape[0]
    for s in range(SUBLANES):
        ref[pl.ds(s, rows, stride=SUBLANES), :] = x[:, s * LANES:(s + 1) * LANES]


def _from_tiles(ref, start, rows):
    return jnp.concatenate([ref[pl.ds(start + s, rows, stride=SUBLANES), :] for s in range(SUBLANES)], axis=1)


def _tile_rows(idx):
    return pl.ds(pl.multiple_of(idx * SUBLANES, SUBLANES), SUBLANES)


def _dispatch_kernel(tail_ref, pos_ref, h_ref, xs_ref, zbuf, src, sem, zsem):
    tt = h_ref.shape[0]
    n_exp = tail_ref.shape[0]
    fill = zbuf.shape[0]

    @pl.when(pl.program_id(0) == 0)
    def _():
        zbuf[...] = jnp.zeros_like(zbuf)

        def tail_copy(e):
            start = pl.multiple_of(tail_ref[e] * SUBLANES, fill)
            return pltpu.make_async_copy(zbuf, xs_ref.at[pl.ds(start, fill), :], zsem)

        def zstart(e, carry):
            tail_copy(e).start()
            return carry

        def zwait(e, carry):
            tail_copy(e).wait()
            return carry

        lax.fori_loop(0, n_exp, zstart, 0)
        lax.fori_loop(0, n_exp, zwait, 0)

    _to_tiles(src, h_ref[...])

    def issue(t, carry):
        for kk in range(TOP_K):
            p = pos_ref[t * SLOTS + kk]
            pltpu.make_async_copy(src.at[_tile_rows(t), :], xs_ref.at[_tile_rows(p), :], sem).start(
                priority=kk % 2)
        return carry

    lax.fori_loop(0, tt, issue, 0)
    for kk in range(TOP_K):
        pltpu.make_async_copy(src, xs_ref.at[pl.ds(0, tt * SUBLANES), :], sem).wait()


def _dispatch(tails, pos_flat, h1, rows_total):
    tokens, d_model = h1.shape
    tt = DISPATCH_TILE
    assert tokens % tt == 0 and d_model == SUBLANES * LANES
    grid_spec = pltpu.PrefetchScalarGridSpec(
        num_scalar_prefetch=1,
        grid=(tokens // tt,),
        in_specs=[pl.BlockSpec((tt * SLOTS,), lambda t, tl: (t,), memory_space=pltpu.SMEM),
                  pl.BlockSpec((tt, d_model), lambda t, tl: (t, 0))],
        out_specs=pl.BlockSpec(memory_space=pl.ANY),
        scratch_shapes=[pltpu.VMEM((EXPERT_ROWS * SUBLANES, LANES), F32),
                        pltpu.VMEM((tt * SUBLANES, LANES), F32),
                        pltpu.SemaphoreType.DMA, pltpu.SemaphoreType.DMA],
    )
    return pl.pallas_call(
        _dispatch_kernel,
        grid_spec=grid_spec,
        out_shape=jax.ShapeDtypeStruct((rows_total * SUBLANES, LANES), F32),
        compiler_params=pltpu.CompilerParams(dimension_semantics=("arbitrary",), vmem_limit_bytes=VMEM_LIMIT),
        name="dispatch",
    )(tails, pos_flat, h1)


def _expert_kernel(be_ref, na_ref, xs_ref, wg_ref, wu_ref, wd_ref, ys_ref, wgu_bf, wd_bf):
    j = pl.program_id(0)
    d_exp = wd_ref.shape[0]

    @pl.when(jnp.logical_or(j == 0, be_ref[j] != be_ref[jnp.maximum(j - 1, 0)]))
    def _():
        wgu_bf[:, :d_exp] = wg_ref[...].astype(BF16)
        wgu_bf[:, d_exp:] = wu_ref[...].astype(BF16)
        wd_bf[...] = wd_ref[...].astype(BF16)

    @pl.when(j < na_ref[0])
    def _():
        rows = xs_ref.shape[0] // SUBLANES
        gu = _dot(_from_tiles(xs_ref, 0, rows).astype(BF16), wgu_bf[...])
        act = _silu(gu[:, :d_exp]) * gu[:, d_exp:]
        _to_tiles(ys_ref, _dot(act.astype(BF16), wd_bf[...]))


def _experts(block_expert, n_active, xs, w_gate, w_up, w_down):
    rows_total = xs.shape[0] // SUBLANES
    d_model = w_down.shape[2]
    eb = EXPERT_ROWS
    n_blocks = rows_total // eb
    d_exp = w_down.shape[1]

    def row_map(j, be, na):
        return (jnp.minimum(j, na[0] - 1), 0)

    def w_map(j, be, na):
        return (be[j], 0, 0)

    grid_spec = pltpu.PrefetchScalarGridSpec(
        num_scalar_prefetch=2,
        grid=(n_blocks,),
        in_specs=[pl.BlockSpec((eb * SUBLANES, LANES), row_map),
                  pl.BlockSpec((None, d_model, d_exp), w_map),
                  pl.BlockSpec((None, d_model, d_exp), w_map),
                  pl.BlockSpec((None, d_exp, d_model), w_map)],
        out_specs=pl.BlockSpec((eb * SUBLANES, LANES), row_map),
        scratch_shapes=[pltpu.VMEM((d_model, 2 * d_exp), BF16), pltpu.VMEM((d_exp, d_model), BF16)],
    )
    return pl.pallas_call(
        _expert_kernel,
        grid_spec=grid_spec,
        out_shape=jax.ShapeDtypeStruct(xs.shape, F32),
        compiler_params=pltpu.CompilerParams(dimension_semantics=("arbitrary",), vmem_limit_bytes=VMEM_LIMIT),
        name="experts",
    )(block_expert, n_active, xs, w_gate, w_up, w_down)


def _combine_kernel(pos_ref, nxt_ref, h_ref, w_ref, wsgu_ref, wsd_ref, l2g_ref, l2b_ref, ys_ref, out_ref,
                    gbuf, sems):
    step = pl.program_id(0)
    n_steps = pl.num_programs(0)
    tt = h_ref.shape[0]
    d_sh = wsd_ref.shape[0]
    cur = step % 2

    def plane(buf, kk):
        return (buf * TOP_K + kk) * tt

    def gather(table_ref, buf):
        def issue(t, carry):
            for kk in range(TOP_K):
                p = table_ref[t * SLOTS + kk]
                pltpu.make_async_copy(ys_ref.at[_tile_rows(p), :], gbuf.at[_tile_rows(plane(buf, kk) + t), :],
                                      sems.at[buf]).start(priority=kk % 2)
            return carry

        lax.fori_loop(0, tt, issue, 0)

    @pl.when(step == 0)
    def _():
        gather(pos_ref, 0)

    @pl.when(step + 1 < n_steps)
    def _():
        gather(nxt_ref, 1 - cur)

    h = h_ref[...]
    gu = _dot(h.astype(BF16), wsgu_ref[...])
    act = _silu(gu[:, :d_sh]) * gu[:, d_sh:]
    acc = DEEPNORM_ALPHA * h + _dot(act.astype(BF16), wsd_ref[...])

    for kk in range(TOP_K):
        first = pl.multiple_of(plane(cur, kk) * SUBLANES, SUBLANES)
        pltpu.make_async_copy(ys_ref.at[pl.ds(0, tt * SUBLANES), :], gbuf.at[pl.ds(first, tt * SUBLANES), :],
                              sems.at[cur]).wait()

    w = w_ref[...]
    for kk in range(TOP_K):
        acc = acc + w[:, kk:kk + 1] * _from_tiles(gbuf, plane(cur, kk) * SUBLANES, tt)
    out_ref[...] = _layer_norm(acc, l2g_ref[...], l2b_ref[...])


def _combine(pos_flat, h1, w_tok, ws_gu, ws_down, l2g, l2b, ys):
    tokens, d_model = h1.shape
    tt = COMBINE_TILE
    assert tokens % tt == 0
    full = lambda a: pl.BlockSpec(a.shape, lambda t: (0,) * a.ndim)
    n_tiles = tokens // tt
    return pl.pallas_call(
        _combine_kernel,
        grid=(n_tiles,),
        in_specs=[pl.BlockSpec((tt * SLOTS,), lambda t: (t,), memory_space=pltpu.SMEM),
                  pl.BlockSpec((tt * SLOTS,), lambda t: (jnp.minimum(t + 1, n_tiles - 1),),
                               memory_space=pltpu.SMEM),
                  pl.BlockSpec((tt, d_model), lambda t: (t, 0)),
                  pl.BlockSpec((tt, SLOTS), lambda t: (t, 0)),
                  full(ws_gu), full(ws_down), full(l2g), full(l2b),
                  pl.BlockSpec(memory_space=pl.ANY)],
        out_specs=pl.BlockSpec((tt, d_model), lambda t: (t, 0)),
        out_shape=jax.ShapeDtypeStruct((tokens, d_model), F32),
        scratch_shapes=[pltpu.VMEM((2 * TOP_K * tt * SUBLANES, LANES), F32), pltpu.SemaphoreType.DMA((2,))],
        compiler_params=pltpu.CompilerParams(dimension_semantics=("arbitrary",), vmem_limit_bytes=VMEM_LIMIT),
        name="combine",
    )(pos_flat, pos_flat, h1, w_tok, ws_gu, ws_down, l2g, l2b, ys)


def kernel(x, meta_tokens, ln_in_g, ln_in_b, w_in, b_in, hg_lb_logits, hg_norm_g, conv_w, conv_b, lru_wa, lru_ba,
           lru_wx, lru_bx, lru_lambda, w_out, ln1_g, ln1_b, router_w, router_bias, we_gate, we_up, we_down,
           ws_gate, ws_up, ws_down, ln2_g, ln2_b):
    batch, seq, d_model = x.shape
    assert w_in.shape[0] == DEPTH
    n_exp = router_w.shape[-1]
    row = lambda a: a.reshape(1, -1).astype(F32)

    meta_pad = jnp.concatenate([jnp.zeros((ROW_BLOCK - N_META, d_model), F32), meta_tokens.astype(F32)], axis=0)
    w_lru = jnp.concatenate([lru_wa[0], lru_wx[0]], axis=-1).astype(BF16)
    h1 = _mixer(x, meta_pad, row(ln_in_g), row(ln_in_b), w_in[0].astype(BF16), row(b_in[0]),
                hg_lb_logits.astype(F32), row(hg_norm_g[0]), conv_w[0].astype(F32), row(conv_b[0]), w_lru,
                row(lru_ba[0]), row(lru_bx[0]), row(lru_lambda[0]), w_out[0].astype(BF16),
                row(ln1_g[0]), row(ln1_b[0]))
    tokens = batch * seq
    h1 = h1.reshape(tokens, d_model)

    e_tab, r_tab, w_tab, counts = _router(h1, router_w[0].astype(F32), router_bias[0].astype(F32))

    eb = EXPERT_ROWS
    n_blocks = tokens * TOP_K // eb + n_exp
    counts = counts[:, 0].astype(jnp.int32)
    padded = (counts + eb - 1) // eb * eb
    pad_ends = jnp.cumsum(padded)
    pad_starts = pad_ends - padded
    n_active = (pad_ends[-1] // eb).astype(jnp.int32)
    blk = jnp.minimum(jnp.arange(n_blocks, dtype=jnp.int32), n_active - 1)
    block_expert = jnp.sum((pad_ends[None, :] <= (blk * eb)[:, None]).astype(jnp.int32), axis=1)
    block_expert = jnp.minimum(block_expert, n_exp - 1)
    pos = _positions(pad_starts.astype(jnp.int32), e_tab, r_tab)
    pos_flat = pos.T.reshape(tokens * SLOTS).astype(jnp.int32)
    tails = jnp.maximum(pad_ends - eb, 0).astype(jnp.int32)

    xs = _dispatch(tails, pos_flat, h1, n_blocks * eb)
    ys = _experts(block_expert, n_active.reshape(1), xs, we_gate[0].astype(F32), we_up[0].astype(F32),
                  we_down[0].astype(F32))
    ws_gu = jnp.concatenate([ws_gate[0], ws_up[0]], axis=-1).astype(BF16)
    out = _combine(pos_flat, h1, w_tab.T, ws_gu, ws_down[0].astype(BF16), row(ln2_g[0]), row(ln2_b[0]), ys)
    return out.reshape(batch, seq, d_model)
```

```python
import jax
import jax.numpy as jnp
from jax import lax
from jax.experimental import pallas as pl
from jax.experimental.pallas import tpu as pltpu

F32 = jnp.float32
BF16 = jnp.bfloat16

N_META = 16
LN_EPS = 1e-5
RMS_EPS = 1e-6
HG_HEADS = 8
HEAD_DIM = 128
LRU_BLOCKS = 8
LRU_BLOCK = 128
CONV_W = 4
LRU_C = 8.0
TOP_K = 6
ROUTED_SCALE = 2.5
DEPTH = 1
DEEPNORM_ALPHA = (2.0 * DEPTH) ** 0.25
LOG2_E = 1.4426950408889634

ROW_BLOCK = 256
CHUNK = 128
ROUTER_TILE = 1024
POSITION_TILE = 4096
DISPATCH_TILE = 1024
COMBINE_TILE = 512
EXPERT_ROWS = 1024
FILL_ROWS = 128
SLOTS = 8
SUBLANES = 8
LANES = 128
VMEM_LIMIT = 56 * 1024 * 1024


def _layer_norm(x, g, b):
    mu = jnp.mean(x, axis=-1, keepdims=True)
    xc = x - mu
    var = jnp.mean(xc * xc, axis=-1, keepdims=True)
    return xc * lax.rsqrt(var + LN_EPS) * g + b


def _sigmoid(x):
    return 1.0 / (1.0 + jnp.exp2(x * (-LOG2_E)))


def _silu(x):
    return x * _sigmoid(x)


def _gelu_tanh(x):
    return 0.5 * x * (1.0 + jnp.tanh(0.7978845608028654 * (x + 0.044715 * (x * x * x))))


def _dot(a, b):
    return jnp.dot(a, b, preferred_element_type=F32)


def _dot_nt(a, b):
    return lax.dot_general(a, b, (((1,), (1,)), ((), ())), preferred_element_type=F32)


def _split3(x):
    hi = x.astype(BF16)
    r1 = x - hi.astype(F32)
    mid = r1.astype(BF16)
    lo = (r1 - mid.astype(F32)).astype(BF16)
    return hi, mid, lo


def _linear_scan(a, u, h0):
    rows, width = a.shape
    groups = rows // SUBLANES
    a3 = a.reshape(groups, SUBLANES, width)
    u3 = u.reshape(groups, SUBLANES, width)
    sid = lax.broadcasted_iota(jnp.int32, (groups, SUBLANES, width), 1)
    d = 1
    while d < SUBLANES:
        keep = sid >= d
        a_sh = jnp.where(keep, pltpu.roll(a3, d, 1), 1.0)
        u_sh = jnp.where(keep, pltpu.roll(u3, d, 1), 0.0)
        u3 = u3 + a3 * u_sh
        a3 = a3 * a_sh
        d *= 2
    out = []
    carry = h0
    for g in range(groups):
        hg = u3[g] + a3[g] * carry
        out.append(hg)
        carry = hg[SUBLANES - 1:SUBLANES, :]
    return jnp.concatenate(out, axis=0)


def _mixer_kernel(x_ref, meta_ref, lng_ref, lnb_ref, win_ref, bin_ref, lbl_ref, hgn_ref, cw_ref, cb_ref,
                  wlru_ref, ba_ref, bx_ref, lam_ref, wout_ref, l1g_ref, l1b_ref,
                  h1_ref,
                  st_ref, cwin_ref, hcar_ref, yhg_ref, st0_ref, cwin0_ref, hcar0_ref):
    bi = pl.program_id(0)
    i = pl.program_id(1)
    is_first = i == 0

    @pl.when(jnp.logical_and(is_first, bi == 0))
    def _():
        st_ref[...] = jnp.zeros_like(st_ref)
        cwin_ref[0:SUBLANES, :] = jnp.zeros_like(cwin0_ref)
        hcar_ref[...] = jnp.zeros_like(hcar_ref)

    @pl.when(jnp.logical_and(is_first, bi > 0))
    def _():
        st_ref[...] = st0_ref[...]
        cwin_ref[0:SUBLANES, :] = cwin0_ref[...]
        hcar_ref[...] = hcar0_ref[...]

    @pl.when(jnp.logical_or(jnp.logical_not(is_first), bi == 0))
    def _():
        _mixer_block(x_ref, meta_ref, lng_ref, lnb_ref, win_ref, bin_ref, lbl_ref, hgn_ref, cw_ref, cb_ref,
                     wlru_ref, ba_ref, bx_ref, lam_ref, wout_ref, l1g_ref, l1b_ref, h1_ref,
                     st_ref, cwin_ref, hcar_ref, yhg_ref, is_first)

        @pl.when(is_first)
        def _():
            st0_ref[...] = st_ref[...]
            cwin0_ref[...] = cwin_ref[0:SUBLANES, :]
            hcar0_ref[...] = hcar_ref[...]


def _mixer_block(x_ref, meta_ref, lng_ref, lnb_ref, win_ref, bin_ref, lbl_ref, hgn_ref, cw_ref, cb_ref,
                 wlru_ref, ba_ref, bx_ref, lam_ref, wout_ref, l1g_ref, l1b_ref, h1_ref,
                 st_ref, cwin_ref, hcar_ref, yhg_ref, is_first):
    rows = x_ref.shape[0]
    d_model = x_ref.shape[1]
    rid = lax.broadcasted_iota(jnp.int32, (rows, 1), 0)
    valid = jnp.logical_or(rid >= rows - N_META, jnp.logical_not(is_first))
    xin = jnp.where(is_first, meta_ref[...], x_ref[...])
    h = _layer_norm(xin, lng_ref[...], lnb_ref[...])
    hb = h.astype(BF16)

    def proj(seg):
        cols = slice(seg * d_model, (seg + 1) * d_model)
        return _dot(hb, win_ref[:, cols]) + bin_ref[:, cols]

    lbl = lbl_ref[...]
    lmax = jnp.max(lbl, axis=0, keepdims=True)
    lexp = jnp.exp(lbl - lmax)
    lb = lexp[0:1, :] / jnp.sum(lexp, axis=0, keepdims=True)

    q = _silu(proj(0))
    f = lb + (1.0 - lb) * _sigmoid(proj(1))
    lf = jnp.where(valid, jnp.log(f), 0.0)
    k = jnp.where(valid, 1.0 - f, 0.0)
    v = proj(2)

    r_i = lax.broadcasted_iota(jnp.int32, (CHUNK, CHUNK), 0)
    c_i = lax.broadcasted_iota(jnp.int32, (CHUNK, CHUNK), 1)
    causal = r_i >= c_i
    tri = jnp.where(causal, 1.0, 0.0).astype(BF16)
    hgn = hgn_ref[...]
    half = CHUNK // 2
    for c in range(rows // CHUNK):
        rs = slice(c * CHUNK, (c + 1) * CHUNK)
        lf_hi, lf_mid, lf_lo = _split3(lf[rs])
        b = _dot(tri, lf_hi) + _dot(tri, lf_mid) + _dot(tri, lf_lo)
        b_mid = b[half - 1:half, :]
        b_last = b[CHUNK - 1:CHUNK, :]
        q_t = q[rs] * jnp.exp(b - b_mid)
        k_t = k[rs] * jnp.exp(b_mid - b)
        q_s = (q_t * jnp.exp(b_mid)).astype(BF16)
        k_s = (k_t * jnp.exp(b_last - b_mid)).astype(BF16)
        q_t = q_t.astype(BF16)
        k_t = k_t.astype(BF16)
        s_decay = jnp.exp(b_last)
        vc = v[rs]
        vb = vc.astype(BF16)
        for hd in range(HG_HEADS):
            cs = slice(hd * HEAD_DIM, (hd + 1) * HEAD_DIM)
            p = jnp.where(causal, _dot_nt(q_t[:, cs], k_t[:, cs]), 0.0)
            st = st_ref[hd]
            o = _dot(p.astype(BF16), vb[:, cs]) + _dot_nt(q_s[:, cs], st.astype(BF16))
            st_ref[hd] = st * s_decay[:, cs] + _dot(vc[:, cs].T.astype(BF16), k_s[:, cs])
            o = o * lax.rsqrt(jnp.mean(o * o, axis=-1, keepdims=True) + RMS_EPS) * hgn
            yhg_ref[rs, cs] = o
    y_hg = yhg_ref[...] * _silu(proj(3))

    lx = jnp.where(valid, proj(4), 0.0)
    cwin_ref[SUBLANES:SUBLANES + rows, :] = lx
    cw = cw_ref[...]
    first_tap = SUBLANES - (CONV_W - 1)
    xc = cb_ref[...] + cw[0:1, :] * cwin_ref[first_tap:first_tap + rows, :]
    for j in range(1, CONV_W):
        xc = xc + cw[j:j + 1, :] * cwin_ref[first_tap + j:first_tap + j + rows, :]
    cwin_ref[0:SUBLANES, :] = lx[rows - SUBLANES:rows, :]

    ra_parts, ri_parts = [], []
    for n in range(LRU_BLOCKS):
        cs = slice(n * LRU_BLOCK, (n + 1) * LRU_BLOCK)
        g2 = _dot(xc[:, cs].astype(BF16), wlru_ref[n])
        ra_parts.append(g2[:, :LRU_BLOCK])
        ri_parts.append(g2[:, LRU_BLOCK:])
    r_gate = _sigmoid(jnp.concatenate(ra_parts, axis=1) + ba_ref[...])
    i_gate = _sigmoid(jnp.concatenate(ri_parts, axis=1) + bx_ref[...])
    neg = -lam_ref[...]
    softplus_neg = jnp.maximum(neg, 0.0) + jnp.log(1.0 + jnp.exp(-jnp.abs(neg)))
    rate = (-LRU_C * LOG2_E) * softplus_neg
    a = jnp.exp2(r_gate * rate)
    u = jnp.where(valid, jnp.sqrt(1.0 - a * a) * (i_gate * xc), 0.0)
    h_lru = _linear_scan(a, u, hcar_ref[0:1, :])
    hcar_ref[0:1, :] = h_lru[rows - 1:rows, :]
    y_lru = h_lru * _gelu_tanh(proj(5))

    merged = _sigmoid(proj(6)) * y_hg + _sigmoid(proj(7)) * y_lru
    mix = _dot(merged.astype(BF16), wout_ref[...])
    h1_ref[...] = _layer_norm(DEEPNORM_ALPHA * h + mix, l1g_ref[...], l1b_ref[...])


def _const_spec(shape):
    nd = len(shape)
    return pl.BlockSpec(shape, lambda b, i: (0,) * nd, pipeline_mode=pl.Buffered(1))


def _mixer(x, meta_pad, ln_g, ln_b, w_in, b_in, lb_logits, hg_norm, conv_w, conv_b, w_lru, ba, bx, lam,
           w_out, l1g, l1b):
    batch, seq, d_model = x.shape
    rows = ROW_BLOCK
    assert seq % rows == 0 and d_model == HG_HEADS * HEAD_DIM == LRU_BLOCKS * LRU_BLOCK
    n_blk = seq // rows + 1
    consts = (meta_pad, ln_g, ln_b, w_in, b_in, lb_logits, hg_norm, conv_w, conv_b, w_lru, ba, bx, lam,
              w_out, l1g, l1b)
    x_spec = pl.BlockSpec((None, rows, d_model), lambda b, i: (b, jnp.maximum(i - 1, 0), 0))
    return pl.pallas_call(
        _mixer_kernel,
        grid=(batch, n_blk),
        in_specs=[x_spec] + [_const_spec(c.shape) for c in consts],
        out_specs=pl.BlockSpec((None, rows, d_model), lambda b, i: (b, jnp.maximum(i - 1, 0), 0)),
        out_shape=jax.ShapeDtypeStruct((batch, seq, d_model), F32),
        scratch_shapes=[
            pltpu.VMEM((HG_HEADS, HEAD_DIM, HEAD_DIM), F32),
            pltpu.VMEM((rows + SUBLANES, d_model), F32),
            pltpu.VMEM((SUBLANES, d_model), F32),
            pltpu.VMEM((rows, d_model), F32),
            pltpu.VMEM((HG_HEADS, HEAD_DIM, HEAD_DIM), F32),
            pltpu.VMEM((SUBLANES, d_model), F32),
            pltpu.VMEM((SUBLANES, d_model), F32),
        ],
        compiler_params=pltpu.CompilerParams(
            dimension_semantics=("arbitrary", "arbitrary"), vmem_limit_bytes=VMEM_LIMIT),
        name="mixer",
    )(x, *consts)


def _router_kernel(h_ref, rw_ref, rb_ref, e_ref, r_ref, w_ref, cnt_ref):
    step = pl.program_id(0)

    @pl.when(step == 0)
    def _():
        cnt_ref[...] = jnp.zeros_like(cnt_ref)

    h = h_ref[...]
    tt = h.shape[0]
    n_exp = cnt_ref.shape[0]
    h_hi = h.astype(BF16)
    h_lo = (h - h_hi.astype(F32)).astype(BF16)
    rw = rw_ref[...]
    w_hi = rw.astype(BF16)
    w_lo = (rw - w_hi.astype(F32)).astype(BF16)
    logits = _dot(h_hi, w_hi) + _dot(h_hi, w_lo) + _dot(h_lo, w_hi)
    scores = _sigmoid(logits.T[0:n_exp, :])
    sel = scores + rb_ref[...]
    eid = lax.broadcasted_iota(jnp.int32, (n_exp, tt), 0)
    onehots, picks = [], []
    for _ in range(TOP_K):
        m = jnp.max(sel, axis=0, keepdims=True)
        idx = jnp.min(jnp.where(sel == m, eid, n_exp), axis=0, keepdims=True)
        oh = eid == idx
        onehots.append(oh)
        picks.append(idx)
        sel = jnp.where(oh, -jnp.inf, sel)
    chosen = onehots[0]
    for oh in onehots[1:]:
        chosen = jnp.logical_or(chosen, oh)
    chosen_f = jnp.where(chosen, 1.0, 0.0)
    r_i = lax.broadcasted_iota(jnp.int32, (tt, tt), 0)
    c_i = lax.broadcasted_iota(jnp.int32, (tt, tt), 1)
    earlier = jnp.where(r_i < c_i, 1.0, 0.0).astype(BF16)
    counts = cnt_ref[...]
    rank = _dot(chosen_f.astype(BF16), earlier) + counts[:, 0:1]
    cnt_ref[...] = counts + jnp.sum(chosen_f, axis=1, keepdims=True)

    s_rows = [jnp.sum(jnp.where(oh, scores, 0.0), axis=0, keepdims=True) for oh in onehots]
    denom = s_rows[0]
    for s in s_rows[1:]:
        denom = denom + s
    for kk, oh in enumerate(onehots):
        e_ref[kk:kk + 1, :] = picks[kk]
        r_ref[kk:kk + 1, :] = jnp.sum(jnp.where(oh, rank, 0.0), axis=0, keepdims=True).astype(jnp.int32)
        w_ref[kk:kk + 1, :] = s_rows[kk] / denom * ROUTED_SCALE
    e_ref[TOP_K:SLOTS, :] = jnp.zeros((SLOTS - TOP_K, tt), jnp.int32)
    r_ref[TOP_K:SLOTS, :] = jnp.zeros((SLOTS - TOP_K, tt), jnp.int32)
    w_ref[TOP_K:SLOTS, :] = jnp.zeros((SLOTS - TOP_K, tt), F32)


def _router(h1, router_w, router_b):
    tokens, d_model = h1.shape
    n_exp = router_w.shape[1]
    tt = ROUTER_TILE
    assert tokens % tt == 0 and n_exp <= LANES and n_exp % SUBLANES == 0
    rw_pad = jnp.pad(router_w, ((0, 0), (0, LANES - n_exp)))
    rb_cols = jnp.broadcast_to(router_b.reshape(n_exp, 1), (n_exp, tt))
    tab = lambda dt: jax.ShapeDtypeStruct((SLOTS, tokens), dt)
    tab_spec = pl.BlockSpec((SLOTS, tt), lambda t: (0, t))
    return pl.pallas_call(
        _router_kernel,
        grid=(tokens // tt,),
        in_specs=[pl.BlockSpec((tt, d_model), lambda t: (t, 0)),
                  pl.BlockSpec((d_model, LANES), lambda t: (0, 0)),
                  pl.BlockSpec((n_exp, tt), lambda t: (0, 0))],
        out_specs=[tab_spec, tab_spec, tab_spec, pl.BlockSpec((n_exp, LANES), lambda t: (0, 0))],
        out_shape=[tab(jnp.int32), tab(jnp.int32), tab(F32), jax.ShapeDtypeStruct((n_exp, LANES), F32)],
        compiler_params=pltpu.CompilerParams(dimension_semantics=("arbitrary",), vmem_limit_bytes=VMEM_LIMIT),
        name="router",
    )(h1, rw_pad, rb_cols)


def _positions_kernel(start_ref, e_ref, r_ref, pos_ref):
    e = e_ref[...]
    pos = r_ref[...]
    for ex in range(start_ref.shape[0]):
        pos = pos + jnp.where(e == ex, start_ref[ex], 0)
    pos_ref[...] = pos


def _positions(pad_starts, e_tab, r_tab):
    slots, tokens = e_tab.shape
    tt = POSITION_TILE
    assert tokens % tt == 0
    spec = pl.BlockSpec((slots, tt), lambda t, st: (0, t))
    grid_spec = pltpu.PrefetchScalarGridSpec(
        num_scalar_prefetch=1, grid=(tokens // tt,), in_specs=[spec, spec], out_specs=spec)
    return pl.pallas_call(
        _positions_kernel,
        grid_spec=grid_spec,
        out_shape=jax.ShapeDtypeStruct((slots, tokens), jnp.int32),
        compiler_params=pltpu.CompilerParams(dimension_semantics=("arbitrary",), vmem_limit_bytes=VMEM_LIMIT),
        name="positions",
    )(pad_starts, e_tab, r_tab)


def _to_tiles(ref, x):
    rows = x.shape[0]
    for s in range(SUBLANES):
        ref[pl.ds(s, rows, stride=SUBLANES), :] = x[:, s * LANES:(s + 1) * LANES]


def _from_tiles(ref, start, rows):
    return jnp.concatenate([ref[pl.ds(start + s, rows, stride=SUBLANES), :] for s in range(SUBLANES)], axis=1)


def _tile_rows(idx):
    return pl.ds(pl.multiple_of(idx * SUBLANES, SUBLANES), SUBLANES)


def _dispatch_kernel(fill_ref, need_ref, pos_ref, h_ref, xs_ref, zbuf, src, tiles, sem, zsem):
    tt = h_ref.shape[0]
    fill = zbuf.shape[0]
    n_fill = fill_ref.shape[0]

    @pl.when(pl.program_id(0) == 0)
    def _():
        zbuf[...] = jnp.zeros_like(zbuf)

        def fill_copy(i):
            start = pl.multiple_of(fill_ref[i], fill)
            return pltpu.make_async_copy(zbuf, xs_ref.at[pl.ds(start, fill)], zsem)

        def zstart(i, carry):
            @pl.when(need_ref[i] == 1)
            def _():
                fill_copy(i).start()
            return carry

        def zwait(i, carry):
            @pl.when(need_ref[i] == 1)
            def _():
                fill_copy(i).wait()
            return carry

        lax.fori_loop(0, n_fill, zstart, 0)
        lax.fori_loop(0, n_fill, zwait, 0)

    _to_tiles(tiles, h_ref[...])
    src[...] = tiles[...].reshape(tt, SUBLANES, LANES).astype(BF16)

    def issue(t, carry):
        for kk in range(TOP_K):
            p = pos_ref[t * SLOTS + kk]
            pltpu.make_async_copy(src.at[t], xs_ref.at[p], sem).start(priority=kk % 2)
        return carry

    lax.fori_loop(0, tt, issue, 0)
    for kk in range(TOP_K):
        pltpu.make_async_copy(src, xs_ref.at[pl.ds(0, tt)], sem).wait()


def _dispatch(fill_starts, fill_needed, pos_flat, h1, rows_total):
    tokens, d_model = h1.shape
    tt = DISPATCH_TILE
    assert tokens % tt == 0 and d_model == SUBLANES * LANES
    grid_spec = pltpu.PrefetchScalarGridSpec(
        num_scalar_prefetch=2,
        grid=(tokens // tt,),
        in_specs=[pl.BlockSpec((tt * SLOTS,), lambda t, fs, nf: (t,), memory_space=pltpu.SMEM),
                  pl.BlockSpec((tt, d_model), lambda t, fs, nf: (t, 0))],
        out_specs=pl.BlockSpec(memory_space=pl.ANY),
        scratch_shapes=[pltpu.VMEM((FILL_ROWS, SUBLANES, LANES), BF16),
                        pltpu.VMEM((tt, SUBLANES, LANES), BF16),
                        pltpu.VMEM((tt * SUBLANES, LANES), F32),
                        pltpu.SemaphoreType.DMA, pltpu.SemaphoreType.DMA],
    )
    return pl.pallas_call(
        _dispatch_kernel,
        grid_spec=grid_spec,
        out_shape=jax.ShapeDtypeStruct((rows_total, SUBLANES, LANES), BF16),
        compiler_params=pltpu.CompilerParams(dimension_semantics=("arbitrary",), vmem_limit_bytes=VMEM_LIMIT),
        name="dispatch",
    )(fill_starts, fill_needed, pos_flat, h1)


def _expert_kernel(be_ref, na_ref, xs_ref, wg_ref, wu_ref, wd_ref, ys_ref, wgu_bf, wd_bf, xtile):
    j = pl.program_id(0)
    d_exp = wd_ref.shape[0]

    @pl.when(jnp.logical_or(j == 0, be_ref[j] != be_ref[jnp.maximum(j - 1, 0)]))
    def _():
        wgu_bf[:, :d_exp] = wg_ref[...].astype(BF16)
        wgu_bf[:, d_exp:] = wu_ref[...].astype(BF16)
        wd_bf[...] = wd_ref[...].astype(BF16)

    @pl.when(j < na_ref[0])
    def _():
        rows = xs_ref.shape[0]
        xtile[...] = xs_ref[...].astype(F32).reshape(rows * SUBLANES, LANES)
        gu = _dot(_from_tiles(xtile, 0, rows).astype(BF16), wgu_bf[...])
        act = _silu(gu[:, :d_exp]) * gu[:, d_exp:]
        _to_tiles(ys_ref, _dot(act.astype(BF16), wd_bf[...]))


def _experts(block_expert, n_active, xs, w_gate, w_up, w_down):
    rows_total = xs.shape[0]
    d_model = w_down.shape[2]
    eb = EXPERT_ROWS
    n_blocks = rows_total // eb
    d_exp = w_down.shape[1]

    def row_map(j, be, na):
        return (jnp.minimum(j, na[0] - 1), 0)

    def w_map(j, be, na):
        return (be[j], 0, 0)

    grid_spec = pltpu.PrefetchScalarGridSpec(
        num_scalar_prefetch=2,
        grid=(n_blocks,),
        in_specs=[pl.BlockSpec((eb, SUBLANES, LANES), lambda j, be, na: row_map(j, be, na) + (0,)),
                  pl.BlockSpec((None, d_model, d_exp), w_map),
                  pl.BlockSpec((None, d_model, d_exp), w_map),
                  pl.BlockSpec((None, d_exp, d_model), w_map)],
        out_specs=pl.BlockSpec((eb * SUBLANES, LANES), row_map),
        scratch_shapes=[pltpu.VMEM((d_model, 2 * d_exp), BF16), pltpu.VMEM((d_exp, d_model), BF16),
                        pltpu.VMEM((eb * SUBLANES, LANES), F32)],
    )
    return pl.pallas_call(
        _expert_kernel,
        grid_spec=grid_spec,
        out_shape=jax.ShapeDtypeStruct((rows_total * SUBLANES, LANES), F32),
        compiler_params=pltpu.CompilerParams(dimension_semantics=("arbitrary",), vmem_limit_bytes=VMEM_LIMIT),
        name="experts",
    )(block_expert, n_active, xs, w_gate, w_up, w_down)


def _combine_kernel(pos_ref, nxt_ref, h_ref, w_ref, wsgu_ref, wsd_ref, l2g_ref, l2b_ref, ys_ref, out_ref,
                    gbuf, sems):
    step = pl.program_id(0)
    n_steps = pl.num_programs(0)
    tt = h_ref.shape[0]
    d_sh = wsd_ref.shape[0]
    cur = step % 2

    def plane(buf, kk):
        return (buf * TOP_K + kk) * tt

    def gather(table_ref, buf):
        def issue(t, carry):
            for kk in range(TOP_K):
                p = table_ref[t * SLOTS + kk]
                pltpu.make_async_copy(ys_ref.at[_tile_rows(p), :], gbuf.at[_tile_rows(plane(buf, kk) + t), :],
                                      sems.at[buf]).start(priority=kk % 2)
            return carry

        lax.fori_loop(0, tt, issue, 0)

    @pl.when(step == 0)
    def _():
        gather(pos_ref, 0)

    @pl.when(step + 1 < n_steps)
    def _():
        gather(nxt_ref, 1 - cur)

    h = h_ref[...]
    gu = _dot(h.astype(BF16), wsgu_ref[...])
    act = _silu(gu[:, :d_sh]) * gu[:, d_sh:]
    acc = DEEPNORM_ALPHA * h + _dot(act.astype(BF16), wsd_ref[...])

    for kk in range(TOP_K):
        first = pl.multiple_of(plane(cur, kk) * SUBLANES, SUBLANES)
        pltpu.make_async_copy(ys_ref.at[pl.ds(0, tt * SUBLANES), :], gbuf.at[pl.ds(first, tt * SUBLANES), :],
                              sems.at[cur]).wait()

    w = w_ref[...]
    for kk in range(TOP_K):
        acc = acc + w[:, kk:kk + 1] * _from_tiles(gbuf, plane(cur, kk) * SUBLANES, tt)
    out_ref[...] = _layer_norm(acc, l2g_ref[...], l2b_ref[...])


def _combine(pos_flat, h1, w_tok, ws_gu, ws_down, l2g, l2b, ys):
    tokens, d_model = h1.shape
    tt = COMBINE_TILE
    assert tokens % tt == 0
    full = lambda a: pl.BlockSpec(a.shape, lambda t: (0,) * a.ndim)
    n_tiles = tokens // tt
    return pl.pallas_call(
        _combine_kernel,
        grid=(n_tiles,),
        in_specs=[pl.BlockSpec((tt * SLOTS,), lambda t: (t,), memory_space=pltpu.SMEM),
                  pl.BlockSpec((tt * SLOTS,), lambda t: (jnp.minimum(t + 1, n_tiles - 1),),
                               memory_space=pltpu.SMEM),
                  pl.BlockSpec((tt, d_model), lambda t: (t, 0)),
                  pl.BlockSpec((tt, SLOTS), lambda t: (t, 0)),
                  full(ws_gu), full(ws_down), full(l2g), full(l2b),
                  pl.BlockSpec(memory_space=pl.ANY)],
        out_specs=pl.BlockSpec((tt, d_model), lambda t: (t, 0)),
        out_shape=jax.ShapeDtypeStruct((tokens, d_model), F32),
        scratch_shapes=[pltpu.VMEM((2 * TOP_K * tt * SUBLANES, LANES), F32), pltpu.SemaphoreType.DMA((2,))],
        compiler_params=pltpu.CompilerParams(dimension_semantics=("arbitrary",), vmem_limit_bytes=VMEM_LIMIT),
        name="combine",
    )(pos_flat, pos_flat, h1, w_tok, ws_gu, ws_down, l2g, l2b, ys)


def kernel(x, meta_tokens, ln_in_g, ln_in_b, w_in, b_in, hg_lb_logits, hg_norm_g, conv_w, conv_b, lru_wa, lru_ba,
           lru_wx, lru_bx, lru_lambda, w_out, ln1_g, ln1_b, router_w, router_bias, we_gate, we_up, we_down,
           ws_gate, ws_up, ws_down, ln2_g, ln2_b):
    batch, seq, d_model = x.shape
    assert w_in.shape[0] == DEPTH
    n_exp = router_w.shape[-1]
    row = lambda a: a.reshape(1, -1).astype(F32)

    meta_pad = jnp.concatenate([jnp.zeros((ROW_BLOCK - N_META, d_model), F32), meta_tokens.astype(F32)], axis=0)
    w_lru = jnp.concatenate([lru_wa[0], lru_wx[0]], axis=-1).astype(BF16)
    h1 = _mixer(x, meta_pad, row(ln_in_g), row(ln_in_b), w_in[0].astype(BF16), row(b_in[0]),
                hg_lb_logits.astype(F32), row(hg_norm_g[0]), conv_w[0].astype(F32), row(conv_b[0]), w_lru,
                row(lru_ba[0]), row(lru_bx[0]), row(lru_lambda[0]), w_out[0].astype(BF16),
                row(ln1_g[0]), row(ln1_b[0]))
    tokens = batch * seq
    h1 = h1.reshape(tokens, d_model)

    e_tab, r_tab, w_tab, counts = _router(h1, router_w[0].astype(F32), router_bias[0].astype(F32))

    eb = EXPERT_ROWS
    n_blocks = tokens * TOP_K // eb + n_exp
    counts = counts[:, 0].astype(jnp.int32)
    padded = (counts + eb - 1) // eb * eb
    pad_ends = jnp.cumsum(padded)
    pad_starts = pad_ends - padded
    n_active = (pad_ends[-1] // eb).astype(jnp.int32)
    blk = jnp.minimum(jnp.arange(n_blocks, dtype=jnp.int32), n_active - 1)
    block_expert = jnp.sum((pad_ends[None, :] <= (blk * eb)[:, None]).astype(jnp.int32), axis=1)
    block_expert = jnp.minimum(block_expert, n_exp - 1)
    pos = _positions(pad_starts.astype(jnp.int32), e_tab, r_tab)
    pos_flat = pos.T.reshape(tokens * SLOTS).astype(jnp.int32)
    c_idx = jnp.arange(eb // FILL_ROWS, dtype=jnp.int32)[None, :]
    fill_needed = (c_idx * FILL_ROWS < (padded - counts)[:, None]).reshape(-1).astype(jnp.int32)
    fill_starts = (pad_ends[:, None] - (c_idx + 1) * FILL_ROWS).reshape(-1).astype(jnp.int32)

    xs = _dispatch(fill_starts, fill_needed, pos_flat, h1, n_blocks * eb)
    ys = _experts(block_expert, n_active.reshape(1), xs, we_gate[0].astype(F32), we_up[0].astype(F32),
                  we_down[0].astype(F32))
    ws_gu = jnp.concatenate([ws_gate[0], ws_up[0]], axis=-1).astype(BF16)
    out = _combine(pos_flat, h1, w_tab.T, ws_gu, ws_down[0].astype(BF16), row(ln2_g[0]), row(ln2_b[0]), ys)
    return out.reshape(batch, seq, d_model)
```

```python
import jax
import jax.numpy as jnp
from jax import lax
from jax.experimental import pallas as pl
from jax.experimental.pallas import tpu as pltpu

F32 = jnp.float32
BF16 = jnp.bfloat16

N_META = 16
LN_EPS = 1e-5
RMS_EPS = 1e-6
HG_HEADS = 8
HEAD_DIM = 128
LRU_BLOCKS = 8
LRU_BLOCK = 128
CONV_W = 4
LRU_C = 8.0
TOP_K = 6
ROUTED_SCALE = 2.5
DEPTH = 1
DEEPNORM_ALPHA = (2.0 * DEPTH) ** 0.25
LOG2_E = 1.4426950408889634

ROW_BLOCK = 256
CHUNK = 128
ROUTER_TILE = 1024
POSITION_TILE = 4096
DISPATCH_TILE = 1024
COMBINE_TILE = 512
EXPERT_ROWS = 1024
FILL_ROWS = 128
SLOTS = 8
SUBLANES = 8
LANES = 128
VMEM_LIMIT = 56 * 1024 * 1024


def _layer_norm(x, g, b):
    mu = jnp.mean(x, axis=-1, keepdims=True)
    xc = x - mu
    var = jnp.mean(xc * xc, axis=-1, keepdims=True)
    return xc * lax.rsqrt(var + LN_EPS) * g + b


def _sigmoid(x):
    return 1.0 / (1.0 + jnp.exp2(x * (-LOG2_E)))


def _silu(x):
    return x * _sigmoid(x)


def _gelu_tanh(x):
    return 0.5 * x * (1.0 + jnp.tanh(0.7978845608028654 * (x + 0.044715 * (x * x * x))))


def _dot(a, b):
    return jnp.dot(a, b, preferred_element_type=F32)


def _dot_nt(a, b):
    return lax.dot_general(a, b, (((1,), (1,)), ((), ())), preferred_element_type=F32)


def _split3(x):
    hi = x.astype(BF16)
    r1 = x - hi.astype(F32)
    mid = r1.astype(BF16)
    lo = (r1 - mid.astype(F32)).astype(BF16)
    return hi, mid, lo


def _linear_scan(a, u, h0):
    rows, width = a.shape
    groups = rows // SUBLANES
    a3 = a.reshape(groups, SUBLANES, width)
    u3 = u.reshape(groups, SUBLANES, width)
    sid = lax.broadcasted_iota(jnp.int32, (groups, SUBLANES, width), 1)
    d = 1
    while d < SUBLANES:
        keep = sid >= d
        a_sh = jnp.where(keep, pltpu.roll(a3, d, 1), 1.0)
        u_sh = jnp.where(keep, pltpu.roll(u3, d, 1), 0.0)
        u3 = u3 + a3 * u_sh
        a3 = a3 * a_sh
        d *= 2
    out = []
    carry = h0
    for g in range(groups):
        hg = u3[g] + a3[g] * carry
        out.append(hg)
        carry = hg[SUBLANES - 1:SUBLANES, :]
    return jnp.concatenate(out, axis=0)


def _mixer_kernel(x_ref, meta_ref, lng_ref, lnb_ref, win_ref, bin_ref, lbl_ref, hgn_ref, cw_ref, cb_ref,
                  wlru_ref, ba_ref, bx_ref, lam_ref, wout_ref, l1g_ref, l1b_ref,
                  h1_ref,
                  st_ref, cwin_ref, hcar_ref, yhg_ref, st0_ref, cwin0_ref, hcar0_ref):
    bi = pl.program_id(0)
    i = pl.program_id(1)
    is_first = i == 0

    @pl.when(jnp.logical_and(is_first, bi == 0))
    def _():
        st_ref[...] = jnp.zeros_like(st_ref)
        cwin_ref[0:SUBLANES, :] = jnp.zeros_like(cwin0_ref)
        hcar_ref[...] = jnp.zeros_like(hcar_ref)

    @pl.when(jnp.logical_and(is_first, bi > 0))
    def _():
        st_ref[...] = st0_ref[...]
        cwin_ref[0:SUBLANES, :] = cwin0_ref[...]
        hcar_ref[...] = hcar0_ref[...]

    @pl.when(jnp.logical_or(jnp.logical_not(is_first), bi == 0))
    def _():
        _mixer_block(x_ref, meta_ref, lng_ref, lnb_ref, win_ref, bin_ref, lbl_ref, hgn_ref, cw_ref, cb_ref,
                     wlru_ref, ba_ref, bx_ref, lam_ref, wout_ref, l1g_ref, l1b_ref, h1_ref,
                     st_ref, cwin_ref, hcar_ref, yhg_ref, is_first)

        @pl.when(is_first)
        def _():
            st0_ref[...] = st_ref[...]
            cwin0_ref[...] = cwin_ref[0:SUBLANES, :]
            hcar0_ref[...] = hcar_ref[...]


def _mixer_block(x_ref, meta_ref, lng_ref, lnb_ref, win_ref, bin_ref, lbl_ref, hgn_ref, cw_ref, cb_ref,
                 wlru_ref, ba_ref, bx_ref, lam_ref, wout_ref, l1g_ref, l1b_ref, h1_ref,
                 st_ref, cwin_ref, hcar_ref, yhg_ref, is_first):
    rows = x_ref.shape[0]
    d_model = x_ref.shape[1]
    rid = lax.broadcasted_iota(jnp.int32, (rows, 1), 0)
    valid = jnp.logical_or(rid >= rows - N_META, jnp.logical_not(is_first))
    xin = jnp.where(is_first, meta_ref[...], x_ref[...])
    h = _layer_norm(xin, lng_ref[...], lnb_ref[...])
    hb = h.astype(BF16)

    def proj(seg):
        cols = slice(seg * d_model, (seg + 1) * d_model)
        return _dot(hb, win_ref[:, cols]) + bin_ref[:, cols]

    lbl = lbl_ref[...]
    lmax = jnp.max(lbl, axis=0, keepdims=True)
    lexp = jnp.exp(lbl - lmax)
    lb = lexp[0:1, :] / jnp.sum(lexp, axis=0, keepdims=True)

    q = _silu(proj(0))
    f = lb + (1.0 - lb) * _sigmoid(proj(1))
    lf = jnp.where(valid, jnp.log(f), 0.0)
    k = jnp.where(valid, 1.0 - f, 0.0)
    v = proj(2)

    r_i = lax.broadcasted_iota(jnp.int32, (CHUNK, CHUNK), 0)
    c_i = lax.broadcasted_iota(jnp.int32, (CHUNK, CHUNK), 1)
    causal = r_i >= c_i
    tri = jnp.where(causal, 1.0, 0.0).astype(BF16)
    hgn = hgn_ref[...]
    half = CHUNK // 2
    for c in range(rows // CHUNK):
        rs = slice(c * CHUNK, (c + 1) * CHUNK)
        lf_hi, lf_mid, lf_lo = _split3(lf[rs])
        b = _dot(tri, lf_hi) + _dot(tri, lf_mid) + _dot(tri, lf_lo)
        b_mid = b[half - 1:half, :]
        b_last = b[CHUNK - 1:CHUNK, :]
        q_t = q[rs] * jnp.exp(b - b_mid)
        k_t = k[rs] * jnp.exp(b_mid - b)
        q_s = (q_t * jnp.exp(b_mid)).astype(BF16)
        k_s = (k_t * jnp.exp(b_last - b_mid)).astype(BF16)
        q_t = q_t.astype(BF16)
        k_t = k_t.astype(BF16)
        s_decay = jnp.exp(b_last)
        vc = v[rs]
        vb = vc.astype(BF16)
        for hd in range(HG_HEADS):
            cs = slice(hd * HEAD_DIM, (hd + 1) * HEAD_DIM)
            p = jnp.where(causal, _dot_nt(q_t[:, cs], k_t[:, cs]), 0.0)
            st = st_ref[hd]
            o = _dot(p.astype(BF16), vb[:, cs]) + _dot_nt(q_s[:, cs], st.astype(BF16))
            st_ref[hd] = st * s_decay[:, cs] + _dot(vc[:, cs].T.astype(BF16), k_s[:, cs])
            o = o * lax.rsqrt(jnp.mean(o * o, axis=-1, keepdims=True) + RMS_EPS) * hgn
            yhg_ref[rs, cs] = o
    y_hg = yhg_ref[...] * _silu(proj(3))

    lx = jnp.where(valid, proj(4), 0.0)
    cwin_ref[SUBLANES:SUBLANES + rows, :] = lx
    cw = cw_ref[...]
    first_tap = SUBLANES - (CONV_W - 1)
    xc = cb_ref[...] + cw[0:1, :] * cwin_ref[first_tap:first_tap + rows, :]
    for j in range(1, CONV_W):
        xc = xc + cw[j:j + 1, :] * cwin_ref[first_tap + j:first_tap + j + rows, :]
    cwin_ref[0:SUBLANES, :] = lx[rows - SUBLANES:rows, :]

    ra_parts, ri_parts = [], []
    for n in range(LRU_BLOCKS):
        cs = slice(n * LRU_BLOCK, (n + 1) * LRU_BLOCK)
        g2 = _dot(xc[:, cs].astype(BF16), wlru_ref[n])
        ra_parts.append(g2[:, :LRU_BLOCK])
        ri_parts.append(g2[:, LRU_BLOCK:])
    r_gate = _sigmoid(jnp.concatenate(ra_parts, axis=1) + ba_ref[...])
    i_gate = _sigmoid(jnp.concatenate(ri_parts, axis=1) + bx_ref[...])
    neg = -lam_ref[...]
    softplus_neg = jnp.maximum(neg, 0.0) + jnp.log(1.0 + jnp.exp(-jnp.abs(neg)))
    rate = (-LRU_C * LOG2_E) * softplus_neg
    a = jnp.exp2(r_gate * rate)
    u = jnp.where(valid, jnp.sqrt(1.0 - a * a) * (i_gate * xc), 0.0)
    h_lru = _linear_scan(a, u, hcar_ref[0:1, :])
    hcar_ref[0:1, :] = h_lru[rows - 1:rows, :]
    y_lru = h_lru * _gelu_tanh(proj(5))

    merged = _sigmoid(proj(6)) * y_hg + _sigmoid(proj(7)) * y_lru
    mix = _dot(merged.astype(BF16), wout_ref[...])
    h1_ref[...] = _layer_norm(DEEPNORM_ALPHA * h + mix, l1g_ref[...], l1b_ref[...])


def _const_spec(shape):
    nd = len(shape)
    return pl.BlockSpec(shape, lambda b, i: (0,) * nd, pipeline_mode=pl.Buffered(1))


def _mixer(x, meta_pad, ln_g, ln_b, w_in, b_in, lb_logits, hg_norm, conv_w, conv_b, w_lru, ba, bx, lam,
           w_out, l1g, l1b):
    batch, seq, d_model = x.shape
    rows = ROW_BLOCK
    assert seq % rows == 0 and d_model == HG_HEADS * HEAD_DIM == LRU_BLOCKS * LRU_BLOCK
    n_blk = seq // rows + 1
    consts = (meta_pad, ln_g, ln_b, w_in, b_in, lb_logits, hg_norm, conv_w, conv_b, w_lru, ba, bx, lam,
              w_out, l1g, l1b)
    x_spec = pl.BlockSpec((None, rows, d_model), lambda b, i: (b, jnp.maximum(i - 1, 0), 0))
    return pl.pallas_call(
        _mixer_kernel,
        grid=(batch, n_blk),
        in_specs=[x_spec] + [_const_spec(c.shape) for c in consts],
        out_specs=pl.BlockSpec((None, rows, d_model), lambda b, i: (b, jnp.maximum(i - 1, 0), 0)),
        out_shape=jax.ShapeDtypeStruct((batch, seq, d_model), F32),
        scratch_shapes=[
            pltpu.VMEM((HG_HEADS, HEAD_DIM, HEAD_DIM), F32),
            pltpu.VMEM((rows + SUBLANES, d_model), F32),
            pltpu.VMEM((SUBLANES, d_model), F32),
            pltpu.VMEM((rows, d_model), F32),
            pltpu.VMEM((HG_HEADS, HEAD_DIM, HEAD_DIM), F32),
            pltpu.VMEM((SUBLANES, d_model), F32),
            pltpu.VMEM((SUBLANES, d_model), F32),
        ],
        compiler_params=pltpu.CompilerParams(
            dimension_semantics=("arbitrary", "arbitrary"), vmem_limit_bytes=VMEM_LIMIT),
        name="mixer",
    )(x, *consts)


def _router_kernel(h_ref, rw_ref, rb_ref, e_ref, r_ref, w_ref, cnt_ref):
    step = pl.program_id(0)

    @pl.when(step == 0)
    def _():
        cnt_ref[...] = jnp.zeros_like(cnt_ref)

    h = h_ref[...]
    tt = h.shape[0]
    n_exp = cnt_ref.shape[0]
    h_hi = h.astype(BF16)
    h_lo = (h - h_hi.astype(F32)).astype(BF16)
    rw = rw_ref[...]
    w_hi = rw.astype(BF16)
    w_lo = (rw - w_hi.astype(F32)).astype(BF16)
    logits = _dot(h_hi, w_hi) + _dot(h_hi, w_lo) + _dot(h_lo, w_hi)
    scores = _sigmoid(logits.T[0:n_exp, :])
    sel = scores + rb_ref[...]
    eid = lax.broadcasted_iota(jnp.int32, (n_exp, tt), 0)
    onehots, picks = [], []
    for _ in range(TOP_K):
        m = jnp.max(sel, axis=0, keepdims=True)
        idx = jnp.min(jnp.where(sel == m, eid, n_exp), axis=0, keepdims=True)
        oh = eid == idx
        onehots.append(oh)
        picks.append(idx)
        sel = jnp.where(oh, -jnp.inf, sel)
    chosen = onehots[0]
    for oh in onehots[1:]:
        chosen = jnp.logical_or(chosen, oh)
    chosen_f = jnp.where(chosen, 1.0, 0.0)
    r_i = lax.broadcasted_iota(jnp.int32, (tt, tt), 0)
    c_i = lax.broadcasted_iota(jnp.int32, (tt, tt), 1)
    earlier = jnp.where(r_i < c_i, 1.0, 0.0).astype(BF16)
    counts = cnt_ref[...]
    rank = _dot(chosen_f.astype(BF16), earlier) + counts[:, 0:1]
    cnt_ref[...] = counts + jnp.sum(chosen_f, axis=1, keepdims=True)

    s_rows = [jnp.sum(jnp.where(oh, scores, 0.0), axis=0, keepdims=True) for oh in onehots]
    denom = s_rows[0]
    for s in s_rows[1:]:
        denom = denom + s
    for kk, oh in enumerate(onehots):
        e_ref[kk:kk + 1, :] = picks[kk]
        r_ref[kk:kk + 1, :] = jnp.sum(jnp.where(oh, rank, 0.0), axis=0, keepdims=True).astype(jnp.int32)
        w_ref[kk:kk + 1, :] = s_rows[kk] / denom * ROUTED_SCALE
    e_ref[TOP_K:SLOTS, :] = jnp.zeros((SLOTS - TOP_K, tt), jnp.int32)
    r_ref[TOP_K:SLOTS, :] = jnp.zeros((SLOTS - TOP_K, tt), jnp.int32)
    w_ref[TOP_K:SLOTS, :] = jnp.zeros((SLOTS - TOP_K, tt), F32)


def _router(h1, router_w, router_b):
    tokens, d_model = h1.shape
    n_exp = router_w.shape[1]
    tt = ROUTER_TILE
    assert tokens % tt == 0 and n_exp <= LANES and n_exp % SUBLANES == 0
    rw_pad = jnp.pad(router_w, ((0, 0), (0, LANES - n_exp)))
    rb_cols = jnp.broadcast_to(router_b.reshape(n_exp, 1), (n_exp, tt))
    tab = lambda dt: jax.ShapeDtypeStruct((SLOTS, tokens), dt)
    tab_spec = pl.BlockSpec((SLOTS, tt), lambda t: (0, t))
    return pl.pallas_call(
        _router_kernel,
        grid=(tokens // tt,),
        in_specs=[pl.BlockSpec((tt, d_model), lambda t: (t, 0)),
                  pl.BlockSpec((d_model, LANES), lambda t: (0, 0)),
                  pl.BlockSpec((n_exp, tt), lambda t: (0, 0))],
        out_specs=[tab_spec, tab_spec, tab_spec, pl.BlockSpec((n_exp, LANES), lambda t: (0, 0))],
        out_shape=[tab(jnp.int32), tab(jnp.int32), tab(F32), jax.ShapeDtypeStruct((n_exp, LANES), F32)],
        compiler_params=pltpu.CompilerParams(dimension_semantics=("arbitrary",), vmem_limit_bytes=VMEM_LIMIT),
        name="router",
    )(h1, rw_pad, rb_cols)


def _positions_kernel(start_ref, e_ref, r_ref, pos_ref):
    e = e_ref[...]
    pos = r_ref[...]
    for ex in range(start_ref.shape[0]):
        pos = pos + jnp.where(e == ex, start_ref[ex], 0)
    pos_ref[...] = pos


def _positions(pad_starts, e_tab, r_tab):
    slots, tokens = e_tab.shape
    tt = POSITION_TILE
    assert tokens % tt == 0
    spec = pl.BlockSpec((slots, tt), lambda t, st: (0, t))
    grid_spec = pltpu.PrefetchScalarGridSpec(
        num_scalar_prefetch=1, grid=(tokens // tt,), in_specs=[spec, spec], out_specs=spec)
    return pl.pallas_call(
        _positions_kernel,
        grid_spec=grid_spec,
        out_shape=jax.ShapeDtypeStruct((slots, tokens), jnp.int32),
        compiler_params=pltpu.CompilerParams(dimension_semantics=("arbitrary",), vmem_limit_bytes=VMEM_LIMIT),
        name="positions",
    )(pad_starts, e_tab, r_tab)


def _to_tiles(ref, x, start=0):
    rows = x.shape[0]
    for s in range(SUBLANES):
        ref[pl.ds(start + s, rows, stride=SUBLANES), :] = x[:, s * LANES:(s + 1) * LANES]


def _from_tiles(ref, start, rows):
    return jnp.concatenate([ref[pl.ds(start + s, rows, stride=SUBLANES), :] for s in range(SUBLANES)], axis=1)


def _tile_rows(idx):
    return pl.ds(pl.multiple_of(idx * SUBLANES, SUBLANES), SUBLANES)


def _dispatch_kernel(fill_ref, need_ref, pos_ref, h_ref, xs_ref, zbuf, src, tiles, sem, zsem):
    tt = h_ref.shape[0]
    fill = zbuf.shape[0]
    n_fill = fill_ref.shape[0]

    @pl.when(pl.program_id(0) == 0)
    def _():
        zbuf[...] = jnp.zeros_like(zbuf)

        def fill_copy(i):
            start = pl.multiple_of(fill_ref[i], fill)
            return pltpu.make_async_copy(zbuf, xs_ref.at[pl.ds(start, fill)], zsem)

        def zstart(i, carry):
            @pl.when(need_ref[i] == 1)
            def _():
                fill_copy(i).start()
            return carry

        def zwait(i, carry):
            @pl.when(need_ref[i] == 1)
            def _():
                fill_copy(i).wait()
            return carry

        lax.fori_loop(0, n_fill, zstart, 0)
        lax.fori_loop(0, n_fill, zwait, 0)

    _to_tiles(tiles, h_ref[...])
    src[...] = tiles[...].reshape(tt, SUBLANES, LANES).astype(BF16)

    def issue(t, carry):
        for kk in range(TOP_K):
            p = pos_ref[t * SLOTS + kk]
            pltpu.make_async_copy(src.at[t], xs_ref.at[p], sem).start(priority=kk % 2)
        return carry

    lax.fori_loop(0, tt, issue, 0)
    for kk in range(TOP_K):
        pltpu.make_async_copy(src, xs_ref.at[pl.ds(0, tt)], sem).wait()


def _dispatch(fill_starts, fill_needed, pos_flat, h1, rows_total):
    tokens, d_model = h1.shape
    tt = DISPATCH_TILE
    assert tokens % tt == 0 and d_model == SUBLANES * LANES
    grid_spec = pltpu.PrefetchScalarGridSpec(
        num_scalar_prefetch=2,
        grid=(tokens // tt,),
        in_specs=[pl.BlockSpec((tt * SLOTS,), lambda t, fs, nf: (t,), memory_space=pltpu.SMEM),
                  pl.BlockSpec((tt, d_model), lambda t, fs, nf: (t, 0))],
        out_specs=pl.BlockSpec(memory_space=pl.ANY),
        scratch_shapes=[pltpu.VMEM((FILL_ROWS, SUBLANES, LANES), BF16),
                        pltpu.VMEM((tt, SUBLANES, LANES), BF16),
                        pltpu.VMEM((tt * SUBLANES, LANES), F32),
                        pltpu.SemaphoreType.DMA, pltpu.SemaphoreType.DMA],
    )
    return pl.pallas_call(
        _dispatch_kernel,
        grid_spec=grid_spec,
        out_shape=jax.ShapeDtypeStruct((rows_total, SUBLANES, LANES), BF16),
        compiler_params=pltpu.CompilerParams(dimension_semantics=("arbitrary",), vmem_limit_bytes=VMEM_LIMIT),
        name="dispatch",
    )(fill_starts, fill_needed, pos_flat, h1)


def _expert_kernel(be_ref, na_ref, xs_ref, wg_ref, wu_ref, wd_ref, ys_ref, wgu_bf, wd_bf, xtile, ybuf, sems):
    j = pl.program_id(0)
    n_active = na_ref[0]
    d_exp = wd_ref.shape[0]
    blk_rows = xtile.shape[0]
    cur = j % 2

    def out_copy(buf, blk):
        src = ybuf.at[pl.ds(pl.multiple_of(buf * blk_rows, blk_rows), blk_rows), :]
        dst = ys_ref.at[pl.ds(pl.multiple_of(blk * blk_rows, blk_rows), blk_rows), :]
        return pltpu.make_async_copy(src, dst, sems.at[buf])

    @pl.when(jnp.logical_or(j == 0, be_ref[j] != be_ref[jnp.maximum(j - 1, 0)]))
    def _():
        wgu_bf[:, :d_exp] = wg_ref[...].astype(BF16)
        wgu_bf[:, d_exp:] = wu_ref[...].astype(BF16)
        wd_bf[...] = wd_ref[...].astype(BF16)

    @pl.when(j < n_active)
    def _():
        @pl.when(j >= 2)
        def _():
            out_copy(cur, j - 2).wait()

        rows = xs_ref.shape[0]
        xtile[...] = xs_ref[...].astype(F32).reshape(rows * SUBLANES, LANES)
        gu = _dot(_from_tiles(xtile, 0, rows).astype(BF16), wgu_bf[...])
        act = _silu(gu[:, :d_exp]) * gu[:, d_exp:]
        _to_tiles(ybuf, _dot(act.astype(BF16), wd_bf[...]), start=cur * blk_rows)
        out_copy(cur, j).start(priority=1)

    @pl.when(j == pl.num_programs(0) - 1)
    def _():
        out_copy((n_active - 1) % 2, n_active - 1).wait()

        @pl.when(n_active >= 2)
        def _():
            out_copy(n_active % 2, n_active - 2).wait()


def _experts(block_expert, n_active, xs, w_gate, w_up, w_down):
    rows_total = xs.shape[0]
    d_model = w_down.shape[2]
    eb = EXPERT_ROWS
    n_blocks = rows_total // eb
    d_exp = w_down.shape[1]

    def row_map(j, be, na):
        return (jnp.minimum(j, na[0] - 1), 0)

    def w_map(j, be, na):
        return (be[j], 0, 0)

    grid_spec = pltpu.PrefetchScalarGridSpec(
        num_scalar_prefetch=2,
        grid=(n_blocks,),
        in_specs=[pl.BlockSpec((eb, SUBLANES, LANES), lambda j, be, na: row_map(j, be, na) + (0,)),
                  pl.BlockSpec((None, d_model, d_exp), w_map),
                  pl.BlockSpec((None, d_model, d_exp), w_map),
                  pl.BlockSpec((None, d_exp, d_model), w_map)],
        out_specs=pl.BlockSpec(memory_space=pl.ANY),
        scratch_shapes=[pltpu.VMEM((d_model, 2 * d_exp), BF16), pltpu.VMEM((d_exp, d_model), BF16),
                        pltpu.VMEM((eb * SUBLANES, LANES), F32),
                        pltpu.VMEM((2 * eb * SUBLANES, LANES), F32), pltpu.SemaphoreType.DMA((2,))],
    )
    return pl.pallas_call(
        _expert_kernel,
        grid_spec=grid_spec,
        out_shape=jax.ShapeDtypeStruct((rows_total * SUBLANES, LANES), F32),
        compiler_params=pltpu.CompilerParams(dimension_semantics=("arbitrary",), vmem_limit_bytes=VMEM_LIMIT),
        name="experts",
    )(block_expert, n_active, xs, w_gate, w_up, w_down)


def _combine_kernel(pos_ref, nxt_ref, h_ref, w_ref, wsgu_ref, wsd_ref, l2g_ref, l2b_ref, ys_ref, out_ref,
                    gbuf, sems):
    step = pl.program_id(0)
    n_steps = pl.num_programs(0)
    tt = h_ref.shape[0]
    d_sh = wsd_ref.shape[0]
    cur = step % 2

    def plane(buf, kk):
        return (buf * TOP_K + kk) * tt

    def gather(table_ref, buf):
        def issue(t, carry):
            for kk in range(TOP_K):
                p = table_ref[t * SLOTS + kk]
                pltpu.make_async_copy(ys_ref.at[_tile_rows(p), :], gbuf.at[_tile_rows(plane(buf, kk) + t), :],
                                      sems.at[buf]).start(priority=kk % 2)
            return carry

        lax.fori_loop(0, tt, issue, 0)

    @pl.when(step == 0)
    def _():
        gather(pos_ref, 0)

    @pl.when(step + 1 < n_steps)
    def _():
        gather(nxt_ref, 1 - cur)

    h = h_ref[...]
    gu = _dot(h.astype(BF16), wsgu_ref[...])
    act = _silu(gu[:, :d_sh]) * gu[:, d_sh:]
    acc = DEEPNORM_ALPHA * h + _dot(act.astype(BF16), wsd_ref[...])

    for kk in range(TOP_K):
        first = pl.multiple_of(plane(cur, kk) * SUBLANES, SUBLANES)
        pltpu.make_async_copy(ys_ref.at[pl.ds(0, tt * SUBLANES), :], gbuf.at[pl.ds(first, tt * SUBLANES), :],
                              sems.at[cur]).wait()

    w = w_ref[...]
    for kk in range(TOP_K):
        acc = acc + w[:, kk:kk + 1] * _from_tiles(gbuf, plane(cur, kk) * SUBLANES, tt)
    out_ref[...] = _layer_norm(acc, l2g_ref[...], l2b_ref[...])


def _combine(pos_flat, h1, w_tok, ws_gu, ws_down, l2g, l2b, ys):
    tokens, d_model = h1.shape
    tt = COMBINE_TILE
    assert tokens % tt == 0
    full = lambda a: pl.BlockSpec(a.shape, lambda t: (0,) * a.ndim)
    n_tiles = tokens // tt
    return pl.pallas_call(
        _combine_kernel,
        grid=(n_tiles,),
        in_specs=[pl.BlockSpec((tt * SLOTS,), lambda t: (t,), memory_space=pltpu.SMEM),
                  pl.BlockSpec((tt * SLOTS,), lambda t: (jnp.minimum(t + 1, n_tiles - 1),),
                               memory_space=pltpu.SMEM),
                  pl.BlockSpec((tt, d_model), lambda t: (t, 0)),
                  pl.BlockSpec((tt, SLOTS), lambda t: (t, 0)),
                  full(ws_gu), full(ws_down), full(l2g), full(l2b),
                  pl.BlockSpec(memory_space=pl.ANY)],
        out_specs=pl.BlockSpec((tt, d_model), lambda t: (t, 0)),
        out_shape=jax.ShapeDtypeStruct((tokens, d_model), F32),
        scratch_shapes=[pltpu.VMEM((2 * TOP_K * tt * SUBLANES, LANES), F32), pltpu.SemaphoreType.DMA((2,))],
        compiler_params=pltpu.CompilerParams(dimension_semantics=("arbitrary",), vmem_limit_bytes=VMEM_LIMIT),
        name="combine",
    )(pos_flat, pos_flat, h1, w_tok, ws_gu, ws_down, l2g, l2b, ys)


def kernel(x, meta_tokens, ln_in_g, ln_in_b, w_in, b_in, hg_lb_logits, hg_norm_g, conv_w, conv_b, lru_wa, lru_ba,
           lru_wx, lru_bx, lru_lambda, w_out, ln1_g, ln1_b, router_w, router_bias, we_gate, we_up, we_down,
           ws_gate, ws_up, ws_down, ln2_g, ln2_b):
    batch, seq, d_model = x.shape
    assert w_in.shape[0] == DEPTH
    n_exp = router_w.shape[-1]
    row = lambda a: a.reshape(1, -1).astype(F32)

    meta_pad = jnp.concatenate([jnp.zeros((ROW_BLOCK - N_META, d_model), F32), meta_tokens.astype(F32)], axis=0)
    w_lru = jnp.concatenate([lru_wa[0], lru_wx[0]], axis=-1).astype(BF16)
    h1 = _mixer(x, meta_pad, row(ln_in_g), row(ln_in_b), w_in[0].astype(BF16), row(b_in[0]),
                hg_lb_logits.astype(F32), row(hg_norm_g[0]), conv_w[0].astype(F32), row(conv_b[0]), w_lru,
                row(lru_ba[0]), row(lru_bx[0]), row(lru_lambda[0]), w_out[0].astype(BF16),
                row(ln1_g[0]), row(ln1_b[0]))
    tokens = batch * seq
    h1 = h1.reshape(tokens, d_model)

    e_tab, r_tab, w_tab, counts = _router(h1, router_w[0].astype(F32), router_bias[0].astype(F32))

    eb = EXPERT_ROWS
    n_blocks = tokens * TOP_K // eb + n_exp
    counts = counts[:, 0].astype(jnp.int32)
    padded = (counts + eb - 1) // eb * eb
    pad_ends = jnp.cumsum(padded)
    pad_starts = pad_ends - padded
    n_active = (pad_ends[-1] // eb).astype(jnp.int32)
    blk = jnp.minimum(jnp.arange(n_blocks, dtype=jnp.int32), n_active - 1)
    block_expert = jnp.sum((pad_ends[None, :] <= (blk * eb)[:, None]).astype(jnp.int32), axis=1)
    block_expert = jnp.minimum(block_expert, n_exp - 1)
    pos = _positions(pad_starts.astype(jnp.int32), e_tab, r_tab)
    pos_flat = pos.T.reshape(tokens * SLOTS).astype(jnp.int32)
    c_idx = jnp.arange(eb // FILL_ROWS, dtype=jnp.int32)[None, :]
    fill_needed = (c_idx * FILL_ROWS < (padded - counts)[:, None]).reshape(-1).astype(jnp.int32)
    fill_starts = (pad_ends[:, None] - (c_idx + 1) * FILL_ROWS).reshape(-1).astype(jnp.int32)

    xs = _dispatch(fill_starts, fill_needed, pos_flat, h1, n_blocks * eb)
    ys = _experts(block_expert, n_active.reshape(1), xs, we_gate[0].astype(F32), we_up[0].astype(F32),
                  we_down[0].astype(F32))
    ws_gu = jnp.concatenate([ws_gate[0], ws_up[0]], axis=-1).astype(BF16)
    out = _combine(pos_flat, h1, w_tab.T, ws_gu, ws_down[0].astype(BF16), row(ln2_g[0]), row(ln2_b[0]), ys)
    return out.reshape(batch, seq, d_model)
```
